```python
import jax, jax.numpy as jnp
from jax import lax
import numpy as np

D_MODEL = 1024
BATCH = 8
SEQ = 4096
DEPTH = 2

N_META = 16
N_FOURIER_GROUPS = 4
FOURIER_GROUP_DIM = D_MODEL // 8
FOURIER_WIDTH = N_FOURIER_GROUPS * FOURIER_GROUP_DIM
N_DELTA_HEADS = 4
HEAD_DIM = D_MODEL // 8
DELTA_WIDTH = N_DELTA_HEADS * HEAD_DIM
N_DIRS = 2
CONV_WIDTH = 5
CHUNK = 64
META_PAD = (-N_META) % CHUNK
D_FF = 4 * D_MODEL
DEEPNORM_ALPHA = (2 * DEPTH) ** 0.25
DEEPNORM_BETA = (8 * DEPTH) ** -0.25
LN_EPS = 1e-5
RMS_EPS = 1e-6
QKV_WIDTH = 3 * DELTA_WIDTH
GATE_WIDTH = 2 * D_MODEL
IN_SPLITS = (FOURIER_WIDTH, QKV_WIDTH, N_DIRS * N_DELTA_HEADS, N_DIRS * N_DELTA_HEADS, DELTA_WIDTH, GATE_WIDTH)
IN_WIDTH = sum(IN_SPLITS)

kernel_name = "fourier_gated_deltanet_deepnorm_encoder"


def layer_norm(x, g, b):
    xf = x.astype(jnp.float32)
    mu = jnp.mean(xf, -1, keepdims=True)
    var = jnp.mean(jnp.square(xf - mu), -1, keepdims=True)
    y = (xf - mu) * lax.rsqrt(var + LN_EPS) * g.astype(jnp.float32) + b.astype(jnp.float32)
    return y.astype(x.dtype)


def l2norm(t):
    return t * lax.rsqrt(jnp.sum(jnp.square(t), -1, keepdims=True) + 1e-6)


def fourier_mix(u):
    Bn, L, _ = u.shape
    ug = u.astype(jnp.float32).reshape(Bn, L, N_FOURIER_GROUPS, FOURIER_GROUP_DIM)
    y = jnp.fft.fft2(ug, axes=(1, 3), norm="ortho").real
    return y.reshape(Bn, L, FOURIER_WIDTH).astype(u.dtype)


def short_conv(x, w):
    C = x.shape[-1]
    y = lax.conv_general_dilated(
        x, w[:, None, :], window_strides=(1,),
        padding=[(CONV_WIDTH // 2, CONV_WIDTH // 2)],
        dimension_numbers=("NWC", "WIO", "NWC"), feature_group_count=C)
    return jax.nn.silu(y)


def chunked_gated_delta_rule(q, k, v, beta, g):
    R, Bn, H, Lp, dk = q.shape
    dv = v.shape[-1]
    N = Lp // CHUNK

    def chunks(t):
        return t.reshape(t.shape[:3] + (N, CHUNK) + t.shape[4:])

    q, k, v, beta, g = chunks(q), chunks(k), chunks(v), chunks(beta), chunks(g)
    g_cum = jnp.cumsum(g, axis=-1)
    idx = jnp.arange(CHUNK)
    lower_incl = idx[:, None] >= idx[None, :]
    lower_strict = idx[:, None] > idx[None, :]
    diff = g_cum[..., :, None] - g_cum[..., None, :]
    decay_incl = jnp.exp(jnp.where(lower_incl, diff, -jnp.inf))
    decay_strict = jnp.where(lower_strict, decay_incl, 0.0)

    k_beta = k * beta[..., None]
    v_beta = v * beta[..., None]
    l_mat = jnp.einsum("rbhncd,rbhnsd->rbhncs", k_beta, k) * decay_strict
    unit_lower = l_mat + jnp.eye(CHUNK, dtype=l_mat.dtype)
    rhs = jnp.concatenate([v_beta, k_beta * jnp.exp(g_cum)[..., None]], axis=-1)
    sol = lax.linalg.triangular_solve(unit_lower, rhs, left_side=True, lower=True, unit_diagonal=True)
    value, k_cumdecay = sol[..., :dv], sol[..., dv:]

    attn = jnp.einsum("rbhncd,rbhnsd->rbhncs", q, k) * decay_incl
    q_decay = q * jnp.exp(g_cum)[..., None]
    g_last = g_cum[..., -1]
    k_tail = k * jnp.exp(g_last[..., None] - g_cum)[..., None]
    chunk_decay = jnp.exp(g_last)

    def to_front(t):
        return jnp.moveaxis(t, 3, 0)

    def step(S, inp):
        val, kcd, qd, at, kt, cd = inp
        v_new = val - jnp.einsum("rbhcd,rbhde->rbhce", kcd, S)
        o = jnp.einsum("rbhcd,rbhde->rbhce", qd, S) + jnp.einsum("rbhcs,rbhse->rbhce", at, v_new)
        S = S * cd[..., None, None] + jnp.einsum("rbhcd,rbhce->rbhde", kt, v_new)
        return S, o

    S0 = jnp.zeros((R, Bn, H, dk, dv), jnp.float32)
    xs = (to_front(value), to_front(k_cumdecay), to_front(q_decay), to_front(attn),
          to_front(k_tail), to_front(chunk_decay))
    _, o = lax.scan(step, S0, xs)
    return jnp.moveaxis(o, 0, 3).reshape(R, Bn, H, Lp, dv)


def gated_delta_branch(qkv, beta_logits, decay_logits, z, conv_w, a_log, dt_bias, norm_g):
    f32 = jnp.float32
    Bn, L, _ = qkv.shape
    qkv = short_conv(qkv.astype(f32), conv_w.astype(f32))
    q, k, v = jnp.split(qkv, 3, axis=-1)

    def heads(t):
        return t.reshape(Bn, L, N_DELTA_HEADS, HEAD_DIM)

    q = l2norm(heads(q)) * HEAD_DIM ** -0.5
    k = l2norm(heads(k))
    v = heads(v)
    beta = jax.nn.sigmoid(beta_logits.astype(f32)).reshape(Bn, L, N_DIRS, N_DELTA_HEADS)
    g = -jnp.exp(a_log.astype(f32)) * jax.nn.softplus(
        decay_logits.astype(f32).reshape(Bn, L, N_DIRS, N_DELTA_HEADS) + dt_bias.astype(f32))

    pad = ((0, 0), (META_PAD, 0), (0, 0), (0, 0))
    q, k, v, beta, g = [jnp.pad(t, pad) for t in (q, k, v, beta, g)]

    def both_dirs(t):
        t = t.transpose(0, 2, 1, 3)
        return jnp.stack([t, jnp.flip(t, 2)])

    def per_dir(t):
        t = t.transpose(2, 0, 3, 1)
        return jnp.stack([t[0], jnp.flip(t[1], -1)])

    o = chunked_gated_delta_rule(both_dirs(q), both_dirs(k), both_dirs(v), per_dir(beta), per_dir(g))
    o = o[0] + jnp.flip(o[1], 2)
    o = o[:, :, META_PAD:].transpose(0, 2, 1, 3)
    zh = z.astype(f32).reshape(Bn, L, N_DELTA_HEADS, HEAD_DIM)
    o = o * lax.rsqrt(jnp.mean(jnp.square(o), -1, keepdims=True) + RMS_EPS) * norm_g.astype(f32) * jax.nn.silu(zh)
    return o.reshape(Bn, L, DELTA_WIDTH).astype(z.dtype)


def hybrid_layer(x, w_in, conv_w, a_log, dt_bias, delta_norm_g, w_fourier_proj, w_delta_proj, w_out,
                 ln1_g, ln1_b, w_up, w_down, ln2_g, ln2_b):
    proj = jnp.einsum("bld,de->ble", x, w_in)
    cuts = np.cumsum(IN_SPLITS)[:-1].tolist()
    u_f, qkv, beta_l, decay_l, z, gates = jnp.split(proj, cuts, axis=-1)
    y_a = jnp.einsum("blf,fd->bld", fourier_mix(u_f), w_fourier_proj)
    y_b = jnp.einsum("blf,fd->bld",
                     gated_delta_branch(qkv, beta_l, decay_l, z, conv_w, a_log, dt_bias, delta_norm_g),
                     w_delta_proj)
    gate_a, gate_b = jnp.split(jax.nn.sigmoid(gates), 2, axis=-1)
    mix = jnp.einsum("bld,de->ble", gate_a * y_a + gate_b * y_b, w_out)
    x = layer_norm(DEEPNORM_ALPHA * x + mix, ln1_g, ln1_b)
    ff = jnp.einsum("blf,fd->bld", jnp.square(jax.nn.relu(jnp.einsum("bld,df->blf", x, w_up))), w_down)
    return layer_norm(DEEPNORM_ALPHA * x + ff, ln2_g, ln2_b)


def setup_inputs(seed: int = 0) -> dict:
    key = jax.random.key(seed)
    ks = jax.random.split(key, 20)
    f32 = jnp.float32

    def nrm(k, shape, scale):
        return jax.random.normal(k, shape, f32) * scale

    x = nrm(ks[0], (BATCH, SEQ, D_MODEL), 1.0)
    meta_tokens = nrm(ks[1], (N_META, D_MODEL), 1.0)
    ln0_g = 1.0 + nrm(ks[2], (D_MODEL,), 0.02)
    ln0_b = nrm(ks[3], (D_MODEL,), 0.02)
    w_in = nrm(ks[4], (DEPTH, D_MODEL, IN_WIDTH), D_MODEL ** -0.5)
    conv_w = nrm(ks[5], (DEPTH, CONV_WIDTH, QKV_WIDTH), CONV_WIDTH ** -0.5)
    a_log = jnp.log(jax.random.uniform(ks[6], (DEPTH, N_DIRS, N_DELTA_HEADS), f32, 1.0, 16.0))
    dt = jnp.exp(jax.random.uniform(ks[7], (DEPTH, N_DIRS, N_DELTA_HEADS), f32,
                                    float(np.log(1e-3)), float(np.log(1e-1))))
    dt_bias = dt + jnp.log(-jnp.expm1(-dt))
    delta_norm_g = 1.0 + nrm(ks[8], (DEPTH, HEAD_DIM), 0.02)
    w_fourier_proj = nrm(ks[9], (DEPTH, FOURIER_WIDTH, D_MODEL), FOURIER_WIDTH ** -0.5 * DEEPNORM_BETA)
    w_delta_proj = nrm(ks[10], (DEPTH, DELTA_WIDTH, D_MODEL), DELTA_WIDTH ** -0.5 * DEEPNORM_BETA)
    w_out = nrm(ks[11], (DEPTH, D_MODEL, D_MODEL), D_MODEL ** -0.5 * DEEPNORM_BETA)
    ln1_g = 1.0 + nrm(ks[12], (DEPTH, D_MODEL), 0.02)
    ln1_b = nrm(ks[13], (DEPTH, D_MODEL), 0.02)
    w_up = nrm(ks[14], (DEPTH, D_MODEL, D_FF), D_MODEL ** -0.5)
    w_down = nrm(ks[15], (DEPTH, D_FF, D_MODEL), D_FF ** -0.5 * DEEPNORM_BETA)
    ln2_g = 1.0 + nrm(ks[16], (DEPTH, D_MODEL), 0.02)
    ln2_b = nrm(ks[17], (DEPTH, D_MODEL), 0.02)
    return {"x": x, "meta_tokens": meta_tokens, "ln0_g": ln0_g, "ln0_b": ln0_b, "w_in": w_in,
            "conv_w": conv_w, "a_log": a_log, "dt_bias": dt_bias, "delta_norm_g": delta_norm_g,
            "w_fourier_proj": w_fourier_proj, "w_delta_proj": w_delta_proj, "w_out": w_out,
            "ln1_g": ln1_g, "ln1_b": ln1_b, "w_up": w_up, "w_down": w_down,
            "ln2_g": ln2_g, "ln2_b": ln2_b}


def reference(x, meta_tokens, ln0_g, ln0_b, w_in, conv_w, a_log, dt_bias, delta_norm_g,
              w_fourier_proj, w_delta_proj, w_out, ln1_g, ln1_b, w_up, w_down, ln2_g, ln2_b):
    Bn = x.shape[0]
    meta = jnp.broadcast_to(meta_tokens[None].astype(x.dtype), (Bn, N_META, x.shape[-1]))
    h = layer_norm(jnp.concatenate([meta, x], axis=1), ln0_g, ln0_b)
    for l in range(DEPTH):
        h = hybrid_layer(h, w_in[l], conv_w[l], a_log[l], dt_bias[l], delta_norm_g[l],
                         w_fourier_proj[l], w_delta_proj[l], w_out[l],
                         ln1_g[l], ln1_b[l], w_up[l], w_down[l], ln2_g[l], ln2_b[l])
    return h[:, N_META:]
```

```python
import functools
import math

import jax
import jax.numpy as jnp
from jax import lax
from jax.experimental import pallas as pl
from jax.experimental.pallas import tpu as pltpu

N_META = 16
N_GROUPS = 4
GROUP_DIM = 128
N_HEADS = 4
HEAD_DIM = 128
N_DIRS = 2
CONV_WIDTH = 5
CHUNK = 64
TAIL = 128
HALO = 8
LN_EPS = 1e-5
RMS_EPS = 1e-6
L2_EPS = 1e-6
FW = N_GROUPS * GROUP_DIM
DW = N_HEADS * HEAD_DIM
NGATE = N_DIRS * N_HEADS
VMEM_LIMIT = 56 * 1024 * 1024

F32 = jnp.float32
BF16 = jnp.bfloat16


def _cparams(*sem):
    return pltpu.CompilerParams(dimension_semantics=sem, vmem_limit_bytes=VMEM_LIMIT)


def _dot(a, b):
    return jnp.dot(a, b, preferred_element_type=F32)


def _dot_nt(a, b):
    return lax.dot_general(a, b, (((1,), (1,)), ((), ())), preferred_element_type=F32)


def _dot_tn(a, b):
    return lax.dot_general(a, b, (((0,), (0,)), ((), ())), preferred_element_type=F32)


def _layer_norm(x, g, b):
    mu = jnp.mean(x, -1, keepdims=True)
    xc = x - mu
    var = jnp.mean(xc * xc, -1, keepdims=True)
    return xc * lax.rsqrt(var + LN_EPS) * g + b


def _sigmoid(x):
    return 1.0 / (1.0 + jnp.exp(-x))


def _softplus(x):
    return jnp.maximum(x, 0.0) + jnp.log1p(jnp.exp(-jnp.abs(x)))


def _pick_tile(n, target, mult):
    best = None
    for t in range(mult, min(n, target) + 1, mult):
        if n % t == 0:
            best = t
    assert best is not None, (n, target, mult)
    return best


def _resident(shape):
    nd = len(shape)
    return pl.BlockSpec(shape, lambda *_: (0,) * nd, pipeline_mode=pl.Buffered(1))


def _ln0_main_kernel(x_ref, g_ref, b_ref, o_ref):
    o_ref[0] = _layer_norm(x_ref[0], g_ref[...], b_ref[...])


def _ln0_tail_kernel(h_any, meta_ref, g_ref, b_ref, o_ref):
    del h_any
    d = o_ref.shape[-1]
    o_ref[0, : TAIL - N_META, :] = jnp.zeros((TAIL - N_META, d), F32)
    o_ref[0, TAIL - N_META:, :] = _layer_norm(meta_ref[...], g_ref[...], b_ref[...])


def _input_norm(x, meta, g, b):
    bn, seq, d = x.shape
    lp = seq + TAIL
    t0 = _pick_tile(seq, 1024, 128)
    g2, b2 = g.reshape(1, d), b.reshape(1, d)
    h = pl.pallas_call(
        _ln0_main_kernel,
        grid=(bn, seq // t0),
        in_specs=[pl.BlockSpec((1, t0, d), lambda i, j: (i, j, 0)),
                  pl.BlockSpec((1, d), lambda i, j: (0, 0)),
                  pl.BlockSpec((1, d), lambda i, j: (0, 0))],
        out_specs=pl.BlockSpec((1, t0, d), lambda i, j: (i, j, 0)),
        out_shape=jax.ShapeDtypeStruct((bn, lp, d), F32),
        compiler_params=_cparams("parallel", "parallel"),
        name="ln0_main",
    )(x, g2, b2)
    return pl.pallas_call(
        _ln0_tail_kernel,
        grid=(bn,),
        in_specs=[pl.BlockSpec(memory_space=pl.ANY),
                  pl.BlockSpec((N_META, d), lambda i: (0, 0)),
                  pl.BlockSpec((1, d), lambda i: (0, 0)),
                  pl.BlockSpec((1, d), lambda i: (0, 0))],
        out_specs=pl.BlockSpec((1, TAIL, d), lambda i: (i, seq // TAIL, 0)),
        out_shape=jax.ShapeDtypeStruct((bn, lp, d), F32),
        input_output_aliases={0: 0},
        compiler_params=_cparams("parallel"),
        name="ln0_tail",
    )(h, meta, g2, b2)


def _inproj_kernel(h_ref, wu_ref, wqkv_ref, wgc_ref, wgr_ref, wz_ref, wg_ref, cg_ref, sg_ref,
                   uc_ref, us_ref, qkv_ref, gcol_ref, grow_ref, z_ref, gate_ref):
    x = h_ref[...].astype(BF16)
    u = _dot(x, wu_ref[...]).astype(BF16)
    half = 2 * GROUP_DIM
    for i in range(FW // half):
        ui = u[:, i * half:(i + 1) * half]
        uc_ref[:, i * half:(i + 1) * half] = _dot(ui, cg_ref[...]).astype(BF16)
        us_ref[:, i * half:(i + 1) * half] = _dot(ui, sg_ref[...]).astype(BF16)
    qkv_ref[...] = _dot(x, wqkv_ref[...]).astype(BF16)
    gcol_ref[...] = _dot(x, wgc_ref[...])
    gr = _dot_nt(wgr_ref[...], x)
    for c in range(grow_ref.shape[0]):
        grow_ref[c] = gr[:, c * CHUNK:(c + 1) * CHUNK]
    z_ref[...] = _dot(x, wz_ref[...]).astype(BF16)
    gate_ref[...] = _sigmoid(_dot(x, wg_ref[...])).astype(BF16)


def _in_projection(h, w_in, cg2, sg2):
    r, d = h.shape
    tm = _pick_tile(r, 512, 128)
    qw = 3 * DW
    o0, o1, o2, o3 = FW, FW + qw, FW + qw + 2 * NGATE, FW + qw + 2 * NGATE + DW
    wb = w_in.astype(BF16)
    wu, wqkv, wgc, wz, wg = wb[:, :o0], wb[:, o0:o1], wb[:, o1:o2], wb[:, o2:o3], wb[:, o3:]
    wgr = wgc.T
    gw = wg.shape[1]
    row = lambda n: pl.BlockSpec((tm, n), lambda i: (i, 0))
    outs = pl.pallas_call(
        _inproj_kernel,
        grid=(r // tm,),
        in_specs=[row(d), _resident(wu.shape), _resident(wqkv.shape), _resident(wgc.shape),
                  _resident(wgr.shape), _resident(wz.shape), _resident(wg.shape),
                  _resident(cg2.shape), _resident(sg2.shape)],
        out_specs=[row(FW), row(FW), row(qw), row(2 * NGATE),
                   pl.BlockSpec((tm // CHUNK, 2 * NGATE, CHUNK), lambda i: (i, 0, 0)),
                   row(DW), row(gw)],
        out_shape=[jax.ShapeDtypeStruct((r, FW), BF16), jax.ShapeDtypeStruct((r, FW), BF16),
                   jax.ShapeDtypeStruct((r, qw), BF16), jax.ShapeDtypeStruct((r, 2 * NGATE), F32),
                   jax.ShapeDtypeStruct((r // CHUNK, 2 * NGATE, CHUNK), F32),
                   jax.ShapeDtypeStruct((r, DW), BF16), jax.ShapeDtypeStruct((r, gw), BF16)],
        compiler_params=_cparams("parallel"),
        name="in_projection",
    )(h, wu, wqkv, wgc, wgr, wz, wg, cg2, sg2)
    return outs


def _positions(seq):
    lp = seq + TAIL
    p = jnp.arange(lp, dtype=jnp.int32)
    return jnp.where(p < seq, p + N_META, jnp.where(p >= lp - N_META, p - (lp - N_META), -1))


def _dft_tables(seq):
    n = seq + N_META
    lp = seq + TAIL
    nblk = lp // 128
    pos = _positions(seq)
    valid = pos >= 0
    pk = jnp.where(valid, pos, 0)
    base = jnp.concatenate([N_META + 128 * jnp.arange(nblk - 1, dtype=jnp.int32),
                            jnp.array([n - (TAIL - N_META)], jnp.int32)])
    w = 2.0 * math.pi / n
    pa = ((pk[:, None] * base[None, :]) % n).astype(F32) * w
    pb = ((pk[:, None] * jnp.arange(128, dtype=jnp.int32)[None, :]) % n).astype(F32) * w
    ca, sa, cb, sb = jnp.cos(pa), jnp.sin(pa), jnp.cos(pb), jnp.sin(pb)
    scale = n ** -0.5
    m = (valid[:, None] & valid[None, :]).astype(F32) * scale
    c = (ca[:, :, None] * cb[:, None, :] - sa[:, :, None] * sb[:, None, :]).reshape(lp, lp) * m
    s = (sa[:, :, None] * cb[:, None, :] + ca[:, :, None] * sb[:, None, :]).reshape(lp, lp) * m
    return c.astype(BF16), (-s).astype(BF16)


def _group_dft_mats():
    i = jnp.arange(GROUP_DIM, dtype=jnp.int32)
    ang = ((i[:, None] * i[None, :]) % GROUP_DIM).astype(F32) * (2.0 * math.pi / GROUP_DIM)
    eye2 = jnp.eye(2, dtype=F32)
    cg = jnp.kron(eye2, jnp.cos(ang)) * GROUP_DIM ** -0.5
    sg = jnp.kron(eye2, jnp.sin(ang)) * GROUP_DIM ** -0.5
    return cg.astype(BF16), sg.astype(BF16)


def _dft_kernel(c_ref, s_ref, a_ref, b_ref, y_ref, acc_ref):
    kt = pl.program_id(2)

    @pl.when(kt == 0)
    def _():
        acc_ref[...] = jnp.zeros_like(acc_ref)

    c = c_ref[...]
    s = s_ref[...]
    for i in range(a_ref.shape[0]):
        acc_ref[i] += _dot(c, a_ref[i]) + _dot(s, b_ref[i])

    @pl.when(kt == pl.num_programs(2) - 1)
    def _():
        y_ref[...] = acc_ref[...].astype(BF16)


def _fourier_positions(ctab, stab, uc, us):
    bn, lp, fw = uc.shape
    nb = 4 if bn % 4 == 0 else (2 if bn % 2 == 0 else 1)
    tm = _pick_tile(lp, 1408, 128)
    tk = _pick_tile(lp, 384, 128)
    return pl.pallas_call(
        _dft_kernel,
        grid=(bn // nb, lp // tm, lp // tk),
        in_specs=[pl.BlockSpec((tm, tk), lambda g, i, k: (i, k)),
                  pl.BlockSpec((tm, tk), lambda g, i, k: (i, k)),
                  pl.BlockSpec((nb, tk, fw), lambda g, i, k: (g, k, 0)),
                  pl.BlockSpec((nb, tk, fw), lambda g, i, k: (g, k, 0))],
        out_specs=pl.BlockSpec((nb, tm, fw), lambda g, i, k: (g, i, 0)),
        out_shape=jax.ShapeDtypeStruct((bn, lp, fw), BF16),
        scratch_shapes=[pltpu.VMEM((nb, tm, fw), F32)],
        compiler_params=_cparams("parallel", "parallel", "arbitrary"),
        name="fourier_positions",
    )(ctab, stab, uc, us)


def _conv_kernel(seq, qm_ref, qp_ref, qn_ref, gcol_ref, grow_ref, w_ref, ac_ref, dc_ref, ar_ref, dr_ref,
                 q_ref, k_ref, v_ref, gco_ref, gro_ref, xx_ref):
    tm = qm_ref.shape[1]
    j = pl.program_id(1)
    lp = seq + TAIL

    def pad_mask(start, n):
        p = start + lax.broadcasted_iota(jnp.int32, (n, 1), 0)
        p = jnp.where(p < 0, p + lp, jnp.where(p >= lp, p - lp, p))
        return (p < seq) | (p >= lp - N_META)

    live = pad_mask(j * tm, tm)
    xx_ref[:HALO, :] = jnp.where(pad_mask(j * tm - HALO, HALO), qp_ref[0].astype(F32), 0.0)
    xx_ref[HALO:HALO + tm, :] = jnp.where(live, qm_ref[0].astype(F32), 0.0)
    xx_ref[HALO + tm:, :] = jnp.where(pad_mask(j * tm + tm, HALO), qn_ref[0].astype(F32), 0.0)

    half = CONV_WIDTH // 2
    for part, dst in enumerate((q_ref, k_ref, v_ref)):
        cols = slice(part * DW, (part + 1) * DW)
        y = None
        for t in range(CONV_WIDTH):
            term = xx_ref[HALO - half + t:HALO - half + t + tm, cols] * w_ref[t:t + 1, cols]
            y = term if y is None else y + term
        y = jnp.where(live, y * _sigmoid(y), 0.0)
        if part == 2:
            dst[0] = y.astype(BF16)
            continue
        scale = HEAD_DIM ** -0.5 if part == 0 else 1.0
        for h in range(N_HEADS):
            lo = h * HEAD_DIM
            yh = y[:, lo:lo + HEAD_DIM]
            yh = yh * (lax.rsqrt(jnp.sum(yh * yh, -1, keepdims=True) + L2_EPS) * scale)
            dst[0, :, lo:lo + HEAD_DIM] = yh.astype(BF16)

    gl = gcol_ref[0]
    beta = jnp.where(live, _sigmoid(gl[:, :NGATE]), 0.0)
    g = jnp.where(live, -jnp.exp(ac_ref[...]) * _softplus(gl[:, NGATE:] + dc_ref[...]), 0.0)
    i0 = lax.broadcasted_iota(jnp.int32, (CHUNK, CHUNK), 0)
    i1 = lax.broadcasted_iota(jnp.int32, (CHUNK, CHUNK), 1)
    tri_lo = (i0 >= i1).astype(BF16)
    tri_up = (i0 <= i1).astype(BF16)
    is_fwd_col = lax.broadcasted_iota(jnp.int32, (1, NGATE), 1) < N_HEADS
    g1 = g.astype(BF16)
    g2 = (g - g1.astype(F32)).astype(BF16)
    g3 = (g - g1.astype(F32) - g2.astype(F32)).astype(BF16)
    for c in range(tm // CHUNK):
        sl = slice(c * CHUNK, (c + 1) * CHUNK)
        pre = _dot(tri_lo, g1[sl]) + _dot(tri_lo, g2[sl]) + _dot(tri_lo, g3[sl])
        suf = _dot(tri_up, g1[sl]) + _dot(tri_up, g2[sl]) + _dot(tri_up, g3[sl])
        gco_ref[0, sl, :NGATE] = beta[sl]
        gco_ref[0, sl, NGATE:] = jnp.where(is_fwd_col, pre, suf)

    nc = grow_ref.shape[0]
    glr = grow_ref[:, NGATE:, :]
    shp = (nc, NGATE, CHUNK)
    pr = j * tm + lax.broadcasted_iota(jnp.int32, shp, 0) * CHUNK + lax.broadcasted_iota(jnp.int32, shp, 2)
    live_r = (pr < seq) | (pr >= lp - N_META)
    gr = jnp.where(live_r, -jnp.exp(ar_ref[...])[None] * _softplus(glr + dr_ref[...][None]), 0.0)
    gr = gr.reshape(nc * NGATE, CHUNK)
    r1 = gr.astype(BF16)
    r2 = (gr - r1.astype(F32)).astype(BF16)
    r3 = (gr - r1.astype(F32) - r2.astype(F32)).astype(BF16)
    pre_r = _dot(r1, tri_up) + _dot(r2, tri_up) + _dot(r3, tri_up)
    suf_r = _dot(r1, tri_lo) + _dot(r2, tri_lo) + _dot(r3, tri_lo)
    is_fwd_row = lax.broadcasted_iota(jnp.int32, shp, 1) < N_HEADS
    gro_ref[...] = jnp.where(is_fwd_row, pre_r.reshape(shp), suf_r.reshape(shp))


def _conv_and_gates(qkv, gcol, grow, conv_w, a_log, dt_bias, seq):
    bn, lp, qw = qkv.shape
    tm = _pick_tile(lp, 1408, 128)
    nt = lp // tm
    nc = tm // CHUNK
    hb = tm // HALO
    nhb = lp // HALO
    a_flat = a_log.reshape(1, NGATE).astype(F32)
    d_flat = dt_bias.reshape(1, NGATE).astype(F32)
    small = lambda shape: pl.BlockSpec(shape, lambda i, j: (0,) * len(shape))
    tok = lambda n: pl.BlockSpec((1, tm, n), lambda i, j: (i, j, 0))
    return pl.pallas_call(
        functools.partial(_conv_kernel, seq),
        grid=(bn, nt),
        in_specs=[tok(qw),
                  pl.BlockSpec((1, HALO, qw), lambda i, j: (i, (j * hb + nhb - 1) % nhb, 0)),
                  pl.BlockSpec((1, HALO, qw), lambda i, j: (i, ((j + 1) * hb) % nhb, 0)),
                  tok(2 * NGATE),
                  pl.BlockSpec((nc, 2 * NGATE, CHUNK), lambda i, j: (i * nt + j, 0, 0)),
                  small((CONV_WIDTH, qw)), small((1, NGATE)), small((1, NGATE)),
                  small((NGATE, 1)), small((NGATE, 1))],
        out_specs=[tok(DW), tok(DW), tok(DW), tok(2 * NGATE),
                   pl.BlockSpec((nc, NGATE, CHUNK), lambda i, j: (i * nt + j, 0, 0))],
        out_shape=[jax.ShapeDtypeStruct((bn, lp, DW), BF16)] * 3
        + [jax.ShapeDtypeStruct((bn, lp, 2 * NGATE), F32),
           jax.ShapeDtypeStruct((bn * lp // CHUNK, NGATE, CHUNK), F32)],
        scratch_shapes=[pltpu.VMEM((tm + 2 * HALO, qw), F32)],
        compiler_params=_cparams("parallel", "parallel"),
        name="conv_gates",
    )(qkv, qkv, qkv, gcol, grow, conv_w.astype(F32), a_flat, d_flat, a_flat.T, d_flat.T)


def _unit_triangular_inverse(lmat, eye, level_masks):
    x = eye - lmat * level_masks[0]
    for m in level_masks[1:]:
        xb = x.astype(BF16)
        t = _dot(xb, (lmat * m).astype(BF16))
        x = x - _dot(t.astype(BF16), xb)
    return x


def _delta_kernel(qf_ref, kf_ref, vf_ref, cf_ref, rf_ref, qb_ref, kb_ref, vb_ref, cb_ref, rb_ref,
                  of_ref, ob_ref, s_ref):
    step = pl.program_id(1)

    @pl.when(step == 0)
    def _():
        s_ref[...] = jnp.zeros_like(s_ref)

    i0 = lax.broadcasted_iota(jnp.int32, (CHUNK, CHUNK), 0)
    i1 = lax.broadcasted_iota(jnp.int32, (CHUNK, CHUNK), 1)
    eye = (i0 == i1).astype(F32)
    level_masks = []
    b = 1
    while b < CHUNK:
        level_masks.append((((i0 // (2 * b)) == (i1 // (2 * b))) & ((i0 // b) != (i1 // b))).astype(F32))
        b *= 2

    dirs = ((qf_ref, kf_ref, vf_ref, cf_ref, rf_ref, of_ref), (qb_ref, kb_ref, vb_ref, cb_ref, rb_ref, ob_ref))
    for r, (q_ref, k_ref, v_ref, c_ref, r_ref, o_ref) in enumerate(dirs):
        incl = (i0 >= i1) if r == 0 else (i0 <= i1)
        strict = (i0 > i1) if r == 0 else (i0 < i1)
        last = CHUNK - 1 if r == 0 else 0
        gcol = c_ref[0]
        grow = r_ref[0]
        for h in range(N_HEADS):
            c = r * N_HEADS + h
            lo = h * HEAD_DIM
            q = q_ref[0, :, lo:lo + HEAD_DIM]
            k = k_ref[0, :, lo:lo + HEAD_DIM]
            v = v_ref[0, :, lo:lo + HEAD_DIM].astype(F32)
            kf32 = k.astype(F32)
            beta = gcol[:, c:c + 1]
            gc = gcol[:, NGATE + c:NGATE + c + 1]
            gr = grow[c:c + 1, :]
            g_last = gcol[last:last + 1, NGATE + c:NGATE + c + 1]
            e_gc = jnp.exp(gc)
            decay = jnp.exp(jnp.where(incl, gc - gr, -1e30))
            kbeta = kf32 * beta
            lmat = _dot_nt(kbeta.astype(BF16), k) * jnp.where(strict, decay, 0.0)
            tinv = _unit_triangular_inverse(lmat, eye, level_masks).astype(BF16)
            rhs = jnp.concatenate([v * beta, kbeta * e_gc], axis=-1).astype(BF16)
            sol = _dot(tinv, rhs)
            value, k_cumdecay = sol[:, :HEAD_DIM], sol[:, HEAD_DIM:]
            attn = (_dot_nt(q, k) * decay).astype(BF16)
            q_decay = (q.astype(F32) * e_gc).astype(BF16)
            k_tail = (kf32 * jnp.exp(g_last - gc)).astype(BF16)

            s_old = s_ref[c]
            sb = s_old.astype(BF16)
            v_new = value - _dot(k_cumdecay.astype(BF16), sb)
            vb16 = v_new.astype(BF16)
            o = _dot(q_decay, sb) + _dot(attn, vb16)
            s_ref[c] = s_old * jnp.exp(g_last) + _dot_tn(k_tail, vb16)
            o_ref[0, :, lo:lo + HEAD_DIM] = o.astype(BF16)


def _delta_rule(q, k, v, gco, gro, seq):
    bn, lp, dw = q.shape
    nch = lp // CHUNK
    nx = seq // CHUNK
    ntail = nch - nx

    def fwd(s):
        return jnp.where(s < ntail, nch - 1 - s, s - ntail)

    def bwd(s):
        return jnp.where(s < nx, nx - 1 - s, s)

    def tok(order):
        return pl.BlockSpec((1, CHUNK, dw), lambda i, s: (i, order(s), 0))

    def col(order):
        return pl.BlockSpec((1, CHUNK, 2 * NGATE), lambda i, s: (i, order(s), 0))

    def row(order):
        return pl.BlockSpec((1, NGATE, CHUNK), lambda i, s: (i * nch + order(s), 0, 0))

    return pl.pallas_call(
        _delta_kernel,
        grid=(bn, nch),
        in_specs=[tok(fwd), tok(fwd), tok(fwd), col(fwd), row(fwd),
                  tok(bwd), tok(bwd), tok(bwd), col(bwd), row(bwd)],
        out_specs=[tok(fwd), tok(bwd)],
        out_shape=[jax.ShapeDtypeStruct((bn, lp, dw), BF16)] * 2,
        scratch_shapes=[pltpu.VMEM((NGATE, HEAD_DIM, HEAD_DIM), F32)],
        compiler_params=_cparams("parallel", "arbitrary"),
        name="delta_rule",
    )(q, k, v, gco, gro, q, k, v, gco, gro)


def _post_kernel(alpha, y_ref, of_ref, ob_ref, z_ref, gate_ref, h_ref, wf_ref, wd_ref, wo_ref,
                 ng_ref, g_ref, b_ref, o_ref, dn_ref):
    d = h_ref.shape[-1]
    ng = ng_ref[...]
    for hh in range(N_HEADS):
        sl = slice(hh * HEAD_DIM, (hh + 1) * HEAD_DIM)
        o = of_ref[:, sl].astype(F32) + ob_ref[:, sl].astype(F32)
        z = z_ref[:, sl].astype(F32)
        o = o * lax.rsqrt(jnp.mean(o * o, -1, keepdims=True) + RMS_EPS) * ng * (z * _sigmoid(z))
        dn_ref[:, sl] = o.astype(BF16)
    y_a = _dot(y_ref[...], wf_ref[...])
    y_b = _dot(dn_ref[...], wd_ref[...])
    m = gate_ref[:, :d].astype(F32) * y_a + gate_ref[:, d:].astype(F32) * y_b
    mix = _dot(m.astype(BF16), wo_ref[...])
    o_ref[...] = _layer_norm(alpha * h_ref[...] + mix, g_ref[...], b_ref[...])


def _merge_and_project(alpha, y, o_f, o_b, z, gates, h, w_f, w_d, w_o, norm_g, ln_g, ln_b):
    r, d = h.shape
    tm = _pick_tile(r, 512, 128)
    row = lambda n: pl.BlockSpec((tm, n), lambda i: (i, 0))
    wf, wd, wo = w_f.astype(BF16), w_d.astype(BF16), w_o.astype(BF16)
    return pl.pallas_call(
        functools.partial(_post_kernel, alpha),
        grid=(r // tm,),
        in_specs=[row(FW), row(DW), row(DW), row(DW), row(2 * d), row(d),
                  _resident(wf.shape), _resident(wd.shape), _resident(wo.shape),
                  _resident((1, HEAD_DIM)), _resident((1, d)), _resident((1, d))],
        out_specs=row(d),
        out_shape=jax.ShapeDtypeStruct((r, d), F32),
        scratch_shapes=[pltpu.VMEM((tm, DW), BF16)],
        compiler_params=_cparams("parallel"),
        name="merge_project",
    )(y, o_f, o_b, z, gates, h, wf, wd, wo, norm_g.reshape(1, HEAD_DIM).astype(F32),
      ln_g.reshape(1, d).astype(F32), ln_b.reshape(1, d).astype(F32))


def _ffn_kernel(alpha, fchunk, x_ref, wu_ref, wd_ref, g_ref, b_ref, o_ref, acc_ref):
    x = x_ref[0]
    xb = x.astype(BF16)
    nf = wu_ref.shape[1] // fchunk
    for f in range(nf):
        sl = slice(f * fchunk, (f + 1) * fchunk)
        a = jnp.maximum(_dot(xb, wu_ref[:, sl]), 0.0)
        part = _dot((a * a).astype(BF16), wd_ref[sl, :])
        if f == 0:
            acc_ref[...] = part
        else:
            acc_ref[...] += part
    o_ref[0] = _layer_norm(alpha * x + acc_ref[...], g_ref[...], b_ref[...])


def _ffn(alpha, x, w_up, w_down, ln_g, ln_b, rows_out):
    bn, lp, d = x.shape
    dff = w_up.shape[1]
    tm = _pick_tile(rows_out, 512, 128)
    wu, wd = w_up.astype(BF16), w_down.astype(BF16)
    tok = pl.BlockSpec((1, tm, d), lambda i, j: (i, j, 0))
    return pl.pallas_call(
        functools.partial(_ffn_kernel, alpha, min(dff, 512)),
        grid=(bn, rows_out // tm),
        in_specs=[tok, _resident(wu.shape), _resident(wd.shape), _resident((1, d)), _resident((1, d))],
        out_specs=tok,
        out_shape=jax.ShapeDtypeStruct((bn, rows_out, d), F32),
        scratch_shapes=[pltpu.VMEM((tm, d), F32)],
        compiler_params=_cparams("parallel", "parallel"),
        name="ffn",
    )(x, wu, wd, ln_g.reshape(1, d).astype(F32), ln_b.reshape(1, d).astype(F32))


def kernel(x, meta_tokens, ln0_g, ln0_b, w_in, conv_w, a_log, dt_bias, delta_norm_g, w_fourier_proj,
           w_delta_proj, w_out, ln1_g, ln1_b, w_up, w_down, ln2_g, ln2_b):
    bn, seq, d = x.shape
    depth = w_in.shape[0]
    assert seq % 128 == 0 and d == 8 * GROUP_DIM
    lp = seq + TAIL
    r = bn * lp
    alpha = (2 * depth) ** 0.25

    h = _input_norm(x, meta_tokens.astype(x.dtype), ln0_g, ln0_b)
    ctab, stab = _dft_tables(seq)
    cg2, sg2 = _group_dft_mats()
    for l in range(depth):
        uc, us, qkv, gcol, grow, z, gates = _in_projection(h.reshape(r, d), w_in[l], cg2, sg2)
        y = _fourier_positions(ctab, stab, uc.reshape(bn, lp, FW), us.reshape(bn, lp, FW))
        q, k, v, gco, gro = _conv_and_gates(qkv.reshape(bn, lp, 3 * DW), gcol.reshape(bn, lp, 2 * NGATE),
                                            grow, conv_w[l], a_log[l], dt_bias[l], seq)
        o_f, o_b = _delta_rule(q, k, v, gco, gro, seq)
        x1 = _merge_and_project(alpha, y.reshape(r, FW), o_f.reshape(r, DW), o_b.reshape(r, DW), z, gates,
                                h.reshape(r, d), w_fourier_proj[l], w_delta_proj[l], w_out[l],
                                delta_norm_g[l], ln1_g[l], ln1_b[l])
        last = l == depth - 1
        if last:
            return _ffn(alpha, x1.reshape(bn, lp, d), w_up[l], w_down[l], ln2_g[l], ln2_b[l], seq)
        h = _ffn(alpha, x1.reshape(1, r, d), w_up[l], w_down[l], ln2_g[l], ln2_b[l], r)
```

```python
import functools
import math

import jax
import jax.numpy as jnp
from jax import lax
from jax.experimental import pallas as pl
from jax.experimental.pallas import tpu as pltpu

N_META = 16
N_GROUPS = 4
GROUP_DIM = 128
N_HEADS = 4
HEAD_DIM = 128
N_DIRS = 2
CONV_WIDTH = 5
CHUNK = 64
TAIL = 128
HALO = 8
LN_EPS = 1e-5
RMS_EPS = 1e-6
L2_EPS = 1e-6
FW = N_GROUPS * GROUP_DIM
DW = N_HEADS * HEAD_DIM
NGATE = N_DIRS * N_HEADS
VMEM_LIMIT = 56 * 1024 * 1024
DELTA_BATCHES_PER_STEP = 4

F32 = jnp.float32
BF16 = jnp.bfloat16


def _cparams(*sem):
    return pltpu.CompilerParams(dimension_semantics=sem, vmem_limit_bytes=VMEM_LIMIT)


def _dot(a, b):
    return jnp.dot(a, b, preferred_element_type=F32)


def _dot_nt(a, b):
    return lax.dot_general(a, b, (((1,), (1,)), ((), ())), preferred_element_type=F32)


def _dot_tn(a, b):
    return lax.dot_general(a, b, (((0,), (0,)), ((), ())), preferred_element_type=F32)


def _layer_norm(x, g, b):
    mu = jnp.mean(x, -1, keepdims=True)
    xc = x - mu
    var = jnp.mean(xc * xc, -1, keepdims=True)
    return xc * lax.rsqrt(var + LN_EPS) * g + b


def _sigmoid(x):
    return 1.0 / (1.0 + jnp.exp(-x))


def _softplus(x):
    return jnp.maximum(x, 0.0) + jnp.log1p(jnp.exp(-jnp.abs(x)))


def _pick_tile(n, target, mult):
    best = None
    for t in range(mult, min(n, target) + 1, mult):
        if n % t == 0:
            best = t
    assert best is not None, (n, target, mult)
    return best


def _resident(shape):
    nd = len(shape)
    return pl.BlockSpec(shape, lambda *_: (0,) * nd, pipeline_mode=pl.Buffered(1))


def _ln0_main_kernel(x_ref, g_ref, b_ref, o_ref):
    o_ref[0] = _layer_norm(x_ref[0], g_ref[...], b_ref[...])


def _ln0_tail_kernel(h_any, meta_ref, g_ref, b_ref, o_ref):
    del h_any
    d = o_ref.shape[-1]
    o_ref[0, : TAIL - N_META, :] = jnp.zeros((TAIL - N_META, d), F32)
    o_ref[0, TAIL - N_META:, :] = _layer_norm(meta_ref[...], g_ref[...], b_ref[...])


def _input_norm(x, meta, g, b):
    bn, seq, d = x.shape
    lp = seq + TAIL
    t0 = _pick_tile(seq, 1024, 128)
    g2, b2 = g.reshape(1, d), b.reshape(1, d)
    h = pl.pallas_call(
        _ln0_main_kernel,
        grid=(bn, seq // t0),
        in_specs=[pl.BlockSpec((1, t0, d), lambda i, j: (i, j, 0)),
                  pl.BlockSpec((1, d), lambda i, j: (0, 0)),
                  pl.BlockSpec((1, d), lambda i, j: (0, 0))],
        out_specs=pl.BlockSpec((1, t0, d), lambda i, j: (i, j, 0)),
        out_shape=jax.ShapeDtypeStruct((bn, lp, d), F32),
        compiler_params=_cparams("parallel", "parallel"),
        name="ln0_main",
    )(x, g2, b2)
    return pl.pallas_call(
        _ln0_tail_kernel,
        grid=(bn,),
        in_specs=[pl.BlockSpec(memory_space=pl.ANY),
                  pl.BlockSpec((N_META, d), lambda i: (0, 0)),
                  pl.BlockSpec((1, d), lambda i: (0, 0)),
                  pl.BlockSpec((1, d), lambda i: (0, 0))],
        out_specs=pl.BlockSpec((1, TAIL, d), lambda i: (i, seq // TAIL, 0)),
        out_shape=jax.ShapeDtypeStruct((bn, lp, d), F32),
        input_output_aliases={0: 0},
        compiler_params=_cparams("parallel"),
        name="ln0_tail",
    )(h, meta, g2, b2)


def _inproj_kernel(h_ref, wu_ref, wqkv_ref, wgc_ref, wgr_ref, wz_ref, wg_ref, cg_ref, sg_ref,
                   uc_ref, us_ref, qkv_ref, gcol_ref, grow_ref, z_ref, gate_ref):
    x = h_ref[...].astype(BF16)
    u = _dot(x, wu_ref[...]).astype(BF16)
    half = 2 * GROUP_DIM
    for i in range(FW // half):
        ui = u[:, i * half:(i + 1) * half]
        uc_ref[:, i * half:(i + 1) * half] = _dot(ui, cg_ref[...]).astype(BF16)
        us_ref[:, i * half:(i + 1) * half] = _dot(ui, sg_ref[...]).astype(BF16)
    qkv_ref[...] = _dot(x, wqkv_ref[...]).astype(BF16)
    gcol_ref[...] = _dot(x, wgc_ref[...])
    gr = _dot_nt(wgr_ref[...], x)
    for c in range(grow_ref.shape[0]):
        grow_ref[c] = gr[:, c * CHUNK:(c + 1) * CHUNK]
    z_ref[...] = _dot(x, wz_ref[...]).astype(BF16)
    gate_ref[...] = _sigmoid(_dot(x, wg_ref[...])).astype(BF16)


def _in_projection(h, w_in, cg2, sg2):
    r, d = h.shape
    tm = _pick_tile(r, 512, 128)
    qw = 3 * DW
    o0, o1, o2, o3 = FW, FW + qw, FW + qw + 2 * NGATE, FW + qw + 2 * NGATE + DW
    wb = w_in.astype(BF16)
    wu, wqkv, wgc, wz, wg = wb[:, :o0], wb[:, o0:o1], wb[:, o1:o2], wb[:, o2:o3], wb[:, o3:]
    wgr = wgc.T
    gw = wg.shape[1]
    row = lambda n: pl.BlockSpec((tm, n), lambda i: (i, 0))
    outs = pl.pallas_call(
        _inproj_kernel,
        grid=(r // tm,),
        in_specs=[row(d), _resident(wu.shape), _resident(wqkv.shape), _resident(wgc.shape),
                  _resident(wgr.shape), _resident(wz.shape), _resident(wg.shape),
                  _resident(cg2.shape), _resident(sg2.shape)],
        out_specs=[row(FW), row(FW), row(qw), row(2 * NGATE),
                   pl.BlockSpec((tm // CHUNK, 2 * NGATE, CHUNK), lambda i: (i, 0, 0)),
                   row(DW), row(gw)],
        out_shape=[jax.ShapeDtypeStruct((r, FW), BF16), jax.ShapeDtypeStruct((r, FW), BF16),
                   jax.ShapeDtypeStruct((r, qw), BF16), jax.ShapeDtypeStruct((r, 2 * NGATE), F32),
                   jax.ShapeDtypeStruct((r // CHUNK, 2 * NGATE, CHUNK), F32),
                   jax.ShapeDtypeStruct((r, DW), BF16), jax.ShapeDtypeStruct((r, gw), BF16)],
        compiler_params=_cparams("parallel"),
        name="in_projection",
    )(h, wu, wqkv, wgc, wgr, wz, wg, cg2, sg2)
    return outs


def _positions(seq):
    lp = seq + TAIL
    p = jnp.arange(lp, dtype=jnp.int32)
    return jnp.where(p < seq, p + N_META, jnp.where(p >= lp - N_META, p - (lp - N_META), -1))


def _dft_tables(seq):
    n = seq + N_META
    lp = seq + TAIL
    nblk = lp // 128
    pos = _positions(seq)
    valid = pos >= 0
    pk = jnp.where(valid, pos, 0)
    base = jnp.concatenate([N_META + 128 * jnp.arange(nblk - 1, dtype=jnp.int32),
                            jnp.array([n - (TAIL - N_META)], jnp.int32)])
    w = 2.0 * math.pi / n
    pa = ((pk[:, None] * base[None, :]) % n).astype(F32) * w
    pb = ((pk[:, None] * jnp.arange(128, dtype=jnp.int32)[None, :]) % n).astype(F32) * w
    ca, sa, cb, sb = jnp.cos(pa), jnp.sin(pa), jnp.cos(pb), jnp.sin(pb)
    scale = n ** -0.5
    m = (valid[:, None] & valid[None, :]).astype(F32) * scale
    c = (ca[:, :, None] * cb[:, None, :] - sa[:, :, None] * sb[:, None, :]).reshape(lp, lp) * m
    s = (sa[:, :, None] * cb[:, None, :] + ca[:, :, None] * sb[:, None, :]).reshape(lp, lp) * m
    return c.astype(BF16), (-s).astype(BF16)


def _group_dft_mats():
    i = jnp.arange(GROUP_DIM, dtype=jnp.int32)
    ang = ((i[:, None] * i[None, :]) % GROUP_DIM).astype(F32) * (2.0 * math.pi / GROUP_DIM)
    eye2 = jnp.eye(2, dtype=F32)
    cg = jnp.kron(eye2, jnp.cos(ang)) * GROUP_DIM ** -0.5
    sg = jnp.kron(eye2, jnp.sin(ang)) * GROUP_DIM ** -0.5
    return cg.astype(BF16), sg.astype(BF16)


def _dft_kernel(c_ref, s_ref, a_ref, b_ref, y_ref, acc_ref):
    kt = pl.program_id(2)

    @pl.when(kt == 0)
    def _():
        acc_ref[...] = jnp.zeros_like(acc_ref)

    c = c_ref[...]
    s = s_ref[...]
    for i in range(a_ref.shape[0]):
        acc_ref[i] += _dot(c, a_ref[i]) + _dot(s, b_ref[i])

    @pl.when(kt == pl.num_programs(2) - 1)
    def _():
        y_ref[...] = acc_ref[...].astype(BF16)


def _fourier_positions(ctab, stab, uc, us):
    bn, lp, fw = uc.shape
    nb = 4 if bn % 4 == 0 else (2 if bn % 2 == 0 else 1)
    tm = _pick_tile(lp, 1408, 128)
    tk = _pick_tile(lp, 384, 128)
    return pl.pallas_call(
        _dft_kernel,
        grid=(bn // nb, lp // tm, lp // tk),
        in_specs=[pl.BlockSpec((tm, tk), lambda g, i, k: (i, k)),
                  pl.BlockSpec((tm, tk), lambda g, i, k: (i, k)),
                  pl.BlockSpec((nb, tk, fw), lambda g, i, k: (g, k, 0)),
                  pl.BlockSpec((nb, tk, fw), lambda g, i, k: (g, k, 0))],
        out_specs=pl.BlockSpec((nb, tm, fw), lambda g, i, k: (g, i, 0)),
        out_shape=jax.ShapeDtypeStruct((bn, lp, fw), BF16),
        scratch_shapes=[pltpu.VMEM((nb, tm, fw), F32)],
        compiler_params=_cparams("parallel", "parallel", "arbitrary"),
        name="fourier_positions",
    )(ctab, stab, uc, us)


def _conv_kernel(seq, qm_ref, qp_ref, qn_ref, gcol_ref, grow_ref, w_ref, ac_ref, dc_ref, ar_ref, dr_ref,
                 q_ref, k_ref, v_ref, gco_ref, gro_ref, xx_ref):
    tm = qm_ref.shape[1]
    j = pl.program_id(1)
    lp = seq + TAIL

    def pad_mask(start, n):
        p = start + lax.broadcasted_iota(jnp.int32, (n, 1), 0)
        p = jnp.where(p < 0, p + lp, jnp.where(p >= lp, p - lp, p))
        return (p < seq) | (p >= lp - N_META)

    live = pad_mask(j * tm, tm)
    xx_ref[:HALO, :] = jnp.where(pad_mask(j * tm - HALO, HALO), qp_ref[0].astype(F32), 0.0)
    xx_ref[HALO:HALO + tm, :] = jnp.where(live, qm_ref[0].astype(F32), 0.0)
    xx_ref[HALO + tm:, :] = jnp.where(pad_mask(j * tm + tm, HALO), qn_ref[0].astype(F32), 0.0)

    half = CONV_WIDTH // 2
    for part, dst in enumerate((q_ref, k_ref, v_ref)):
        cols = slice(part * DW, (part + 1) * DW)
        y = None
        for t in range(CONV_WIDTH):
            term = xx_ref[HALO - half + t:HALO - half + t + tm, cols] * w_ref[t:t + 1, cols]
            y = term if y is None else y + term
        y = jnp.where(live, y * _sigmoid(y), 0.0)
        if part == 2:
            dst[0] = y.astype(BF16)
            continue
        scale = HEAD_DIM ** -0.5 if part == 0 else 1.0
        for h in range(N_HEADS):
            lo = h * HEAD_DIM
            yh = y[:, lo:lo + HEAD_DIM]
            yh = yh * (lax.rsqrt(jnp.sum(yh * yh, -1, keepdims=True) + L2_EPS) * scale)
            dst[0, :, lo:lo + HEAD_DIM] = yh.astype(BF16)

    gl = gcol_ref[0]
    beta = jnp.where(live, _sigmoid(gl[:, :NGATE]), 0.0)
    g = jnp.where(live, -jnp.exp(ac_ref[...]) * _softplus(gl[:, NGATE:] + dc_ref[...]), 0.0)
    i0 = lax.broadcasted_iota(jnp.int32, (CHUNK, CHUNK), 0)
    i1 = lax.broadcasted_iota(jnp.int32, (CHUNK, CHUNK), 1)
    tri_lo = (i0 >= i1).astype(BF16)
    tri_up = (i0 <= i1).astype(BF16)
    is_fwd_col = lax.broadcasted_iota(jnp.int32, (1, NGATE), 1) < N_HEADS
    g1 = g.astype(BF16)
    g2 = (g - g1.astype(F32)).astype(BF16)
    g3 = (g - g1.astype(F32) - g2.astype(F32)).astype(BF16)
    for c in range(tm // CHUNK):
        sl = slice(c * CHUNK, (c + 1) * CHUNK)
        pre = _dot(tri_lo, g1[sl]) + _dot(tri_lo, g2[sl]) + _dot(tri_lo, g3[sl])
        suf = _dot(tri_up, g1[sl]) + _dot(tri_up, g2[sl]) + _dot(tri_up, g3[sl])
        gco_ref[0, sl, :NGATE] = beta[sl]
        gco_ref[0, sl, NGATE:] = jnp.where(is_fwd_col, pre, suf)

    nc = grow_ref.shape[0]
    glr = grow_ref[:, NGATE:, :]
    shp = (nc, NGATE, CHUNK)
    pr = j * tm + lax.broadcasted_iota(jnp.int32, shp, 0) * CHUNK + lax.broadcasted_iota(jnp.int32, shp, 2)
    live_r = (pr < seq) | (pr >= lp - N_META)
    gr = jnp.where(live_r, -jnp.exp(ar_ref[...])[None] * _softplus(glr + dr_ref[...][None]), 0.0)
    gr = gr.reshape(nc * NGATE, CHUNK)
    r1 = gr.astype(BF16)
    r2 = (gr - r1.astype(F32)).astype(BF16)
    r3 = (gr - r1.astype(F32) - r2.astype(F32)).astype(BF16)
    pre_r = _dot(r1, tri_up) + _dot(r2, tri_up) + _dot(r3, tri_up)
    suf_r = _dot(r1, tri_lo) + _dot(r2, tri_lo) + _dot(r3, tri_lo)
    is_fwd_row = lax.broadcasted_iota(jnp.int32, shp, 1) < N_HEADS
    gro_ref[...] = jnp.where(is_fwd_row, pre_r.reshape(shp), suf_r.reshape(shp))


def _conv_and_gates(qkv, gcol, grow, conv_w, a_log, dt_bias, seq):
    bn, lp, qw = qkv.shape
    tm = _pick_tile(lp, 1408, 128)
    nt = lp // tm
    nc = tm // CHUNK
    hb = tm // HALO
    nhb = lp // HALO
    a_flat = a_log.reshape(1, NGATE).astype(F32)
    d_flat = dt_bias.reshape(1, NGATE).astype(F32)
    small = lambda shape: pl.BlockSpec(shape, lambda i, j: (0,) * len(shape))
    tok = lambda n: pl.BlockSpec((1, tm, n), lambda i, j: (i, j, 0))
    return pl.pallas_call(
        functools.partial(_conv_kernel, seq),
        grid=(bn, nt),
        in_specs=[tok(qw),
                  pl.BlockSpec((1, HALO, qw), lambda i, j: (i, (j * hb + nhb - 1) % nhb, 0)),
                  pl.BlockSpec((1, HALO, qw), lambda i, j: (i, ((j + 1) * hb) % nhb, 0)),
                  tok(2 * NGATE),
                  pl.BlockSpec((nc, 2 * NGATE, CHUNK), lambda i, j: (i * nt + j, 0, 0)),
                  small((CONV_WIDTH, qw)), small((1, NGATE)), small((1, NGATE)),
                  small((NGATE, 1)), small((NGATE, 1))],
        out_specs=[tok(DW), tok(DW), tok(DW), tok(2 * NGATE),
                   pl.BlockSpec((nc, NGATE, CHUNK), lambda i, j: (i * nt + j, 0, 0))],
        out_shape=[jax.ShapeDtypeStruct((bn, lp, DW), BF16)] * 3
        + [jax.ShapeDtypeStruct((bn, lp, 2 * NGATE), F32),
           jax.ShapeDtypeStruct((bn * lp // CHUNK, NGATE, CHUNK), F32)],
        scratch_shapes=[pltpu.VMEM((tm + 2 * HALO, qw), F32)],
        compiler_params=_cparams("parallel", "parallel"),
        name="conv_gates",
    )(qkv, qkv, qkv, gcol, grow, conv_w.astype(F32), a_flat, d_flat, a_flat.T, d_flat.T)


class _Chain:
    pass


def _delta_kernel(qf_ref, kf_ref, vf_ref, cf_ref, rf_ref, qb_ref, kb_ref, vb_ref, cb_ref, rb_ref,
                  of_ref, ob_ref, s_ref):
    step = pl.program_id(1)
    nb = qf_ref.shape[0]

    @pl.when(step == 0)
    def _():
        s_ref[...] = jnp.zeros_like(s_ref)

    i0 = lax.broadcasted_iota(jnp.int32, (CHUNK, CHUNK), 0)
    i1 = lax.broadcasted_iota(jnp.int32, (CHUNK, CHUNK), 1)
    eye = (i0 == i1).astype(F32)
    level_masks = []
    b = 1
    while b < CHUNK:
        level_masks.append((((i0 // (2 * b)) == (i1 // (2 * b))) & ((i0 // b) != (i1 // b))).astype(F32))
        b *= 2

    dirs = ((qf_ref, kf_ref, vf_ref, cf_ref, rf_ref, of_ref), (qb_ref, kb_ref, vb_ref, cb_ref, rb_ref, ob_ref))
    chains = []
    for bi in range(nb):
        for r, (q_ref, k_ref, v_ref, c_ref, r_ref, o_ref) in enumerate(dirs):
            incl = (i0 >= i1) if r == 0 else (i0 <= i1)
            strict = (i0 > i1) if r == 0 else (i0 < i1)
            last = CHUNK - 1 if r == 0 else 0
            gcol = c_ref[bi]
            grow = r_ref[bi, 0]
            for h in range(N_HEADS):
                ch = _Chain()
                c = r * N_HEADS + h
                lo = h * HEAD_DIM
                ch.o_ref, ch.bi, ch.lo, ch.slot = o_ref, bi, lo, bi * NGATE + c
                ch.q = q_ref[bi, :, lo:lo + HEAD_DIM]
                ch.k = k_ref[bi, :, lo:lo + HEAD_DIM]
                v = v_ref[bi, :, lo:lo + HEAD_DIM].astype(F32)
                kf32 = ch.k.astype(F32)
                beta = gcol[:, c:c + 1]
                gc = gcol[:, NGATE + c:NGATE + c + 1]
                gr = grow[c:c + 1, :]
                g_last = gcol[last:last + 1, NGATE + c:NGATE + c + 1]
                e_gc = jnp.exp(gc)
                ch.decay = jnp.exp(jnp.where(incl, gc - gr, -1e30))
                ch.decay_strict = jnp.where(strict, ch.decay, 0.0)
                kbeta = kf32 * beta
                ch.kbeta = kbeta.astype(BF16)
                ch.rhs = jnp.concatenate([v * beta, kbeta * e_gc], axis=-1).astype(BF16)
                ch.q_decay = (ch.q.astype(F32) * e_gc).astype(BF16)
                ch.k_tail = (kf32 * jnp.exp(g_last - gc)).astype(BF16)
                ch.chunk_decay = jnp.exp(g_last)
                chains.append(ch)

    for ch in chains:
        both = _dot_nt(jnp.concatenate([ch.kbeta, ch.q], axis=0), ch.k)
        ch.lmat = both[:CHUNK] * ch.decay_strict
        ch.attn = (both[CHUNK:] * ch.decay).astype(BF16)
        ch.x = eye - ch.lmat * level_masks[0]
    for m in level_masks[1:]:
        for ch in chains:
            ch.xb = ch.x.astype(BF16)
            ch.t = _dot(ch.xb, (ch.lmat * m).astype(BF16)).astype(BF16)
        for ch in chains:
            ch.x = ch.x - _dot(ch.t, ch.xb)
    for ch in chains:
        ch.sol = _dot(ch.x.astype(BF16), ch.rhs)
    for ch in chains:
        ch.s_old = s_ref[ch.slot]
        ch.sb = ch.s_old.astype(BF16)
        lhs = jnp.concatenate([ch.sol[:, HEAD_DIM:].astype(BF16), ch.q_decay], axis=0)
        ch.ks = _dot(lhs, ch.sb)
    for ch in chains:
        v_new = (ch.sol[:, :HEAD_DIM] - ch.ks[:CHUNK]).astype(BF16)
        o = ch.ks[CHUNK:] + _dot(ch.attn, v_new)
        s_ref[ch.slot] = ch.s_old * ch.chunk_decay + _dot_tn(ch.k_tail, v_new)
        ch.o_ref[ch.bi, :, ch.lo:ch.lo + HEAD_DIM] = o.astype(BF16)


def _delta_rule(q, k, v, gco, gro, seq):
    bn, lp, dw = q.shape
    nch = lp // CHUNK
    nx = seq // CHUNK
    ntail = nch - nx

    def fwd(s):
        return jnp.where(s < ntail, nch - 1 - s, s - ntail)

    def bwd(s):
        return jnp.where(s < nx, nx - 1 - s, s)

    nb = DELTA_BATCHES_PER_STEP if bn % DELTA_BATCHES_PER_STEP == 0 else 1
    gro = gro.reshape(bn, nch, NGATE, CHUNK)

    def tok(order):
        return pl.BlockSpec((nb, CHUNK, dw), lambda i, s: (i, order(s), 0))

    def col(order):
        return pl.BlockSpec((nb, CHUNK, 2 * NGATE), lambda i, s: (i, order(s), 0))

    def row(order):
        return pl.BlockSpec((nb, 1, NGATE, CHUNK), lambda i, s: (i, order(s), 0, 0))

    return pl.pallas_call(
        _delta_kernel,
        grid=(bn // nb, nch),
        in_specs=[tok(fwd), tok(fwd), tok(fwd), col(fwd), row(fwd),
                  tok(bwd), tok(bwd), tok(bwd), col(bwd), row(bwd)],
        out_specs=[tok(fwd), tok(bwd)],
        out_shape=[jax.ShapeDtypeStruct((bn, lp, dw), BF16)] * 2,
        scratch_shapes=[pltpu.VMEM((nb * NGATE, HEAD_DIM, HEAD_DIM), F32)],
        compiler_params=_cparams("parallel", "arbitrary"),
        name="delta_rule",
    )(q, k, v, gco, gro, q, k, v, gco, gro)


def _post_kernel(alpha, y_ref, of_ref, ob_ref, z_ref, gate_ref, h_ref, wf_ref, wd_ref, wo_ref,
                 ng_ref, g_ref, b_ref, o_ref, dn_ref):
    d = h_ref.shape[-1]
    ng = ng_ref[...]
    for hh in range(N_HEADS):
        sl = slice(hh * HEAD_DIM, (hh + 1) * HEAD_DIM)
        o = of_ref[:, sl].astype(F32) + ob_ref[:, sl].astype(F32)
        z = z_ref[:, sl].astype(F32)
        o = o * lax.rsqrt(jnp.mean(o * o, -1, keepdims=True) + RMS_EPS) * ng * (z * _sigmoid(z))
        dn_ref[:, sl] = o.astype(BF16)
    y_a = _dot(y_ref[...], wf_ref[...])
    y_b = _dot(dn_ref[...], wd_ref[...])
    m = gate_ref[:, :d].astype(F32) * y_a + gate_ref[:, d:].astype(F32) * y_b
    mix = _dot(m.astype(BF16), wo_ref[...])
    o_ref[...] = _layer_norm(alpha * h_ref[...] + mix, g_ref[...], b_ref[...])


def _merge_and_project(alpha, y, o_f, o_b, z, gates, h, w_f, w_d, w_o, norm_g, ln_g, ln_b):
    r, d = h.shape
    tm = _pick_tile(r, 512, 128)
    row = lambda n: pl.BlockSpec((tm, n), lambda i: (i, 0))
    wf, wd, wo = w_f.astype(BF16), w_d.astype(BF16), w_o.astype(BF16)
    return pl.pallas_call(
        functools.partial(_post_kernel, alpha),
        grid=(r // tm,),
        in_specs=[row(FW), row(DW), row(DW), row(DW), row(2 * d), row(d),
                  _resident(wf.shape), _resident(wd.shape), _resident(wo.shape),
                  _resident((1, HEAD_DIM)), _resident((1, d)), _resident((1, d))],
        out_specs=row(d),
        out_shape=jax.ShapeDtypeStruct((r, d), F32),
        scratch_shapes=[pltpu.VMEM((tm, DW), BF16)],
        compiler_params=_cparams("parallel"),
        name="merge_project",
    )(y, o_f, o_b, z, gates, h, wf, wd, wo, norm_g.reshape(1, HEAD_DIM).astype(F32),
      ln_g.reshape(1, d).astype(F32), ln_b.reshape(1, d).astype(F32))


def _ffn_kernel(alpha, fchunk, x_ref, wu_ref, wd_ref, g_ref, b_ref, o_ref, acc_ref):
    x = x_ref[0]
    xb = x.astype(BF16)
    nf = wu_ref.shape[1] // fchunk
    for f in range(nf):
        sl = slice(f * fchunk, (f + 1) * fchunk)
        a = jnp.maximum(_dot(xb, wu_ref[:, sl]), 0.0)
        part = _dot((a * a).astype(BF16), wd_ref[sl, :])
        if f == 0:
            acc_ref[...] = part
        else:
            acc_ref[...] += part
    o_ref[0] = _layer_norm(alpha * x + acc_ref[...], g_ref[...], b_ref[...])


def _ffn(alpha, x, w_up, w_down, ln_g, ln_b, rows_out):
    bn, lp, d = x.shape
    dff = w_up.shape[1]
    tm = _pick_tile(rows_out, 512, 128)
    wu, wd = w_up.astype(BF16), w_down.astype(BF16)
    tok = pl.BlockSpec((1, tm, d), lambda i, j: (i, j, 0))
    return pl.pallas_call(
        functools.partial(_ffn_kernel, alpha, min(dff, 512)),
        grid=(bn, rows_out // tm),
        in_specs=[tok, _resident(wu.shape), _resident(wd.shape), _resident((1, d)), _resident((1, d))],
        out_specs=tok,
        out_shape=jax.ShapeDtypeStruct((bn, rows_out, d), F32),
        scratch_shapes=[pltpu.VMEM((tm, d), F32)],
        compiler_params=_cparams("parallel", "parallel"),
        name="ffn",
    )(x, wu, wd, ln_g.reshape(1, d).astype(F32), ln_b.reshape(1, d).astype(F32))


def kernel(x, meta_tokens, ln0_g, ln0_b, w_in, conv_w, a_log, dt_bias, delta_norm_g, w_fourier_proj,
           w_delta_proj, w_out, ln1_g, ln1_b, w_up, w_down, ln2_g, ln2_b):
    bn, seq, d = x.shape
    depth = w_in.shape[0]
    assert seq % 128 == 0 and d == 8 * GROUP_DIM
    lp = seq + TAIL
    r = bn * lp
    alpha = (2 * depth) ** 0.25

    h = _input_norm(x, meta_tokens.astype(x.dtype), ln0_g, ln0_b)
    ctab, stab = _dft_tables(seq)
    cg2, sg2 = _group_dft_mats()
    for l in range(depth):
        uc, us, qkv, gcol, grow, z, gates = _in_projection(h.reshape(r, d), w_in[l], cg2, sg2)
        y = _fourier_positions(ctab, stab, uc.reshape(bn, lp, FW), us.reshape(bn, lp, FW))
        q, k, v, gco, gro = _conv_and_gates(qkv.reshape(bn, lp, 3 * DW), gcol.reshape(bn, lp, 2 * NGATE),
                                            grow, conv_w[l], a_log[l], dt_bias[l], seq)
        o_f, o_b = _delta_rule(q, k, v, gco, gro, seq)
        x1 = _merge_and_project(alpha, y.reshape(r, FW), o_f.reshape(r, DW), o_b.reshape(r, DW), z, gates,
                                h.reshape(r, d), w_fourier_proj[l], w_delta_proj[l], w_out[l],
                                delta_norm_g[l], ln1_g[l], ln1_b[l])
        last = l == depth - 1
        if last:
            return _ffn(alpha, x1.reshape(bn, lp, d), w_up[l], w_down[l], ln2_g[l], ln2_b[l], seq)
        h = _ffn(alpha, x1.reshape(1, r, d), w_up[l], w_down[l], ln2_g[l], ln2_b[l], r)
```

```python
import functools
import math

import jax
import jax.numpy as jnp
from jax import lax
from jax.experimental import pallas as pl
from jax.experimental.pallas import tpu as pltpu

N_META = 16
N_GROUPS = 4
GROUP_DIM = 128
N_HEADS = 4
HEAD_DIM = 128
N_DIRS = 2
CONV_WIDTH = 5
CHUNK = 64
TAIL = 128
HALO = 16
LN_EPS = 1e-5
RMS_EPS = 1e-6
L2_EPS = 1e-6
FW = N_GROUPS * GROUP_DIM
DW = N_HEADS * HEAD_DIM
NGATE = N_DIRS * N_HEADS
VMEM_LIMIT = 56 * 1024 * 1024
DELTA_BATCHES_PER_STEP = 4

F32 = jnp.float32
BF16 = jnp.bfloat16


def _cparams(*sem):
    return pltpu.CompilerParams(dimension_semantics=sem, vmem_limit_bytes=VMEM_LIMIT)


def _dot(a, b):
    return jnp.dot(a, b, preferred_element_type=F32)


def _dot_nt(a, b):
    return lax.dot_general(a, b, (((1,), (1,)), ((), ())), preferred_element_type=F32)


def _dot_tn(a, b):
    return lax.dot_general(a, b, (((0,), (0,)), ((), ())), preferred_element_type=F32)


def _layer_norm(x, g, b):
    mu = jnp.mean(x, -1, keepdims=True)
    xc = x - mu
    var = jnp.mean(xc * xc, -1, keepdims=True)
    return xc * lax.rsqrt(var + LN_EPS) * g + b


def _sigmoid(x):
    return 1.0 / (1.0 + jnp.exp(-x))


def _softplus(x):
    return jnp.maximum(x, 0.0) + jnp.log1p(jnp.exp(-jnp.abs(x)))


def _pick_tile(n, target, mult):
    best = None
    for t in range(mult, min(n, target) + 1, mult):
        if n % t == 0:
            best = t
    assert best is not None, (n, target, mult)
    return best


def _resident(shape):
    nd = len(shape)
    return pl.BlockSpec(shape, lambda *_: (0,) * nd, pipeline_mode=pl.Buffered(1))


def _ln0_kernel(x_ref, meta_ref, g_ref, b_ref, o_ref):
    j = pl.program_id(1)
    t0, d = x_ref.shape[1], x_ref.shape[2]
    seq = o_ref.shape[1] - TAIL
    o_ref[0, pl.ds(pl.multiple_of(j * t0, t0), t0), :] = _layer_norm(x_ref[0], g_ref[...], b_ref[...])

    @pl.when(j == 0)
    def _():
        o_ref[0, seq:seq + TAIL - N_META, :] = jnp.zeros((TAIL - N_META, d), F32)
        o_ref[0, seq + TAIL - N_META:, :] = _layer_norm(meta_ref[...], g_ref[...], b_ref[...])


def _input_norm(x, meta, g, b):
    bn, seq, d = x.shape
    lp = seq + TAIL
    t0 = _pick_tile(seq, 1024, 128)
    return pl.pallas_call(
        _ln0_kernel,
        grid=(bn, seq // t0),
        in_specs=[pl.BlockSpec((1, t0, d), lambda i, j: (i, j, 0)),
                  pl.BlockSpec((N_META, d), lambda i, j: (0, 0)),
                  pl.BlockSpec((1, d), lambda i, j: (0, 0)),
                  pl.BlockSpec((1, d), lambda i, j: (0, 0))],
        out_specs=pl.BlockSpec((1, lp, d), lambda i, j: (i, 0, 0)),
        out_shape=jax.ShapeDtypeStruct((bn, lp, d), F32),
        compiler_params=_cparams("parallel", "arbitrary"),
        name="ln0",
    )(x, meta, g.reshape(1, d), b.reshape(1, d))


def _inproj_kernel(h_ref, wu_ref, wqkv_ref, wgc_ref, wgr_ref, wz_ref, wg_ref, cg_ref, sg_ref,
                   uc_ref, us_ref, qkv_ref, gcol_ref, grow_ref, z_ref, gate_ref):
    x = h_ref[...].astype(BF16)
    u = _dot(x, wu_ref[...]).astype(BF16)
    half = 2 * GROUP_DIM
    for i in range(FW // half):
        ui = u[:, i * half:(i + 1) * half]
        uc_ref[:, i * half:(i + 1) * half] = _dot(ui, cg_ref[...]).astype(BF16)
        us_ref[:, i * half:(i + 1) * half] = _dot(ui, sg_ref[...]).astype(BF16)
    qkv_ref[...] = _dot(x, wqkv_ref[...]).astype(BF16)
    gcol_ref[...] = _dot(x, wgc_ref[...])
    gr = _dot_nt(wgr_ref[...], x)
    for c in range(grow_ref.shape[0]):
        grow_ref[c] = gr[:, c * CHUNK:(c + 1) * CHUNK]
    z_ref[...] = _dot(x, wz_ref[...]).astype(BF16)
    gate_ref[...] = _sigmoid(_dot(x, wg_ref[...])).astype(BF16)


def _in_projection(h, w_in, cg2, sg2):
    r, d = h.shape
    tm = _pick_tile(r, 512, 128)
    qw = 3 * DW
    o0, o1, o2, o3 = FW, FW + qw, FW + qw + 2 * NGATE, FW + qw + 2 * NGATE + DW
    wb = w_in.astype(BF16)
    wu, wqkv, wgc, wz, wg = wb[:, :o0], wb[:, o0:o1], wb[:, o1:o2], wb[:, o2:o3], wb[:, o3:]
    wgr = wgc.T
    gw = wg.shape[1]
    row = lambda n: pl.BlockSpec((tm, n), lambda i: (i, 0))
    outs = pl.pallas_call(
        _inproj_kernel,
        grid=(r // tm,),
        in_specs=[row(d), _resident(wu.shape), _resident(wqkv.shape), _resident(wgc.shape),
                  _resident(wgr.shape), _resident(wz.shape), _resident(wg.shape),
                  _resident(cg2.shape), _resident(sg2.shape)],
        out_specs=[row(FW), row(FW), row(qw), row(2 * NGATE),
                   pl.BlockSpec((tm // CHUNK, 2 * NGATE, CHUNK), lambda i: (i, 0, 0)),
                   row(DW), row(gw)],
        out_shape=[jax.ShapeDtypeStruct((r, FW), BF16), jax.ShapeDtypeStruct((r, FW), BF16),
                   jax.ShapeDtypeStruct((r, qw), BF16), jax.ShapeDtypeStruct((r, 2 * NGATE), F32),
                   jax.ShapeDtypeStruct((r // CHUNK, 2 * NGATE, CHUNK), F32),
                   jax.ShapeDtypeStruct((r, DW), BF16), jax.ShapeDtypeStruct((r, gw), BF16)],
        compiler_params=_cparams("parallel"),
        name="in_projection",
    )(h, wu, wqkv, wgc, wgr, wz, wg, cg2, sg2)
    return outs


def _half_rows(seq):
    return seq // 2 + 256


def _dft_tables(seq):
    n = seq + N_META
    hl = _half_rows(seq)
    nblk = hl // 128
    nxb = seq // 256
    m = jnp.arange(hl, dtype=jnp.int32)
    self_row = seq // 2 - N_META // 2
    pos = jnp.where(m < seq // 2, m + N_META, m - seq // 2)
    valid = (m <= self_row) | ((m >= seq // 2) & (m < seq // 2 + N_META))
    weight = jnp.where(valid, jnp.where(m == self_row, 0.5, 1.0), 0.0).astype(F32)
    pk = jnp.where(valid, pos, 0)
    blk = jnp.arange(nblk, dtype=jnp.int32)
    base = jnp.where(blk < nxb, N_META + 128 * blk, 0)
    w = 2.0 * math.pi / n
    pa = ((pk[:, None] * base[None, :]) % n).astype(F32) * w
    pb = ((pk[:, None] * jnp.arange(128, dtype=jnp.int32)[None, :]) % n).astype(F32) * w
    ca, sa, cb, sb = jnp.cos(pa), jnp.sin(pa), jnp.cos(pb), jnp.sin(pb)
    scale = (weight[:, None] * weight[None, :]) * n ** -0.5
    c = (ca[:, :, None] * cb[:, None, :] - sa[:, :, None] * sb[:, None, :]).reshape(hl, hl) * scale
    s = (sa[:, :, None] * cb[:, None, :] + ca[:, :, None] * sb[:, None, :]).reshape(hl, hl) * scale
    return c.astype(BF16), s.astype(BF16)


def _group_dft_mats():
    i = jnp.arange(GROUP_DIM, dtype=jnp.int32)
    ang = ((i[:, None] * i[None, :]) % GROUP_DIM).astype(F32) * (2.0 * math.pi / GROUP_DIM)
    eye2 = jnp.eye(2, dtype=F32)
    cg = jnp.kron(eye2, jnp.cos(ang)) * GROUP_DIM ** -0.5
    sg = jnp.kron(eye2, jnp.sin(ang)) * GROUP_DIM ** -0.5
    return cg.astype(BF16), sg.astype(BF16)


def _mirror_select():
    r = lax.broadcasted_iota(jnp.int32, (128, 256), 0)
    c = lax.broadcasted_iota(jnp.int32, (128, 256), 1)
    rm = lax.broadcasted_iota(jnp.int32, (N_META, 128), 0)
    cm = lax.broadcasted_iota(jnp.int32, (N_META, 128), 1)
    return (c == 240 - r).astype(BF16), ((cm == 128 - rm) & (rm > 0)).astype(BF16)


def _fold_kernel(a_ref, b_ref, e_ref, o_ref):
    lp = a_ref.shape[1]
    seq = lp - TAIL
    sel, sel_meta = _mirror_select()
    for src, dst, sign in ((a_ref, e_ref, 1.0), (b_ref, o_ref, -1.0)):
        for i in range(seq // 256):
            win = seq - 256 - 128 * i
            rev = _dot(sel, src[0, win:win + 256, :])
            dst[0, 128 * i:128 * (i + 1), :] = (src[0, 128 * i:128 * (i + 1), :].astype(F32) + sign * rev).astype(BF16)
        rev = _dot(sel_meta, src[0, seq - 128:seq, :])
        xh = seq // 2
        dst[0, xh:xh + N_META, :] = (src[0, lp - N_META:, :].astype(F32) + sign * rev).astype(BF16)
        dst[0, xh + N_META:, :] = jnp.zeros((dst.shape[1] - xh - N_META, dst.shape[2]), BF16)


def _unfold_kernel(d_ref, s_ref, y_ref):
    lp = y_ref.shape[1]
    seq = lp - TAIL
    fw = y_ref.shape[2]
    nxb = seq // 256
    xh = seq // 2
    sel, sel_meta = _mirror_select()
    for j in range(seq // 128):
        y = d_ref[0, 128 * j:128 * (j + 1), :].astype(F32) if j < nxb else None
        i1 = seq // 128 - 2 - j
        ok1, ok2 = 0 <= i1 < nxb, 0 <= i1 + 1 < nxb
        if ok1 and ok2:
            hi = _dot(sel, s_ref[0, 128 * i1:128 * i1 + 256, :])
        elif ok1:
            hi = _dot(sel[:, :128], s_ref[0, 128 * i1:128 * i1 + 128, :])
        elif ok2:
            hi = _dot(sel[:, 128:], s_ref[0, 128 * (i1 + 1):128 * (i1 + 2), :])
        else:
            hi = None
        if j == seq // 128 - 1:
            hi = hi + _dot_tn(sel_meta, s_ref[0, xh:xh + N_META, :])
        y = hi if y is None else (y if hi is None else y + hi)
        y_ref[0, 128 * j:128 * (j + 1), :] = y.astype(BF16)
    y_ref[0, seq:lp - N_META, :] = jnp.zeros((TAIL - N_META, fw), BF16)
    y_ref[0, lp - N_META:, :] = d_ref[0, xh:xh + N_META, :]


def _dft_kernel(c_ref, s_ref, e_ref, o_ref, d_ref, m_ref, p_acc, q_acc):
    kt = pl.program_id(2)

    @pl.when(kt == 0)
    def _():
        p_acc[...] = jnp.zeros_like(p_acc)
        q_acc[...] = jnp.zeros_like(q_acc)

    c = c_ref[...]
    s = s_ref[...]
    for i in range(e_ref.shape[0]):
        p_acc[i] += _dot(c, e_ref[i])
        q_acc[i] += _dot(s, o_ref[i])

    @pl.when(kt == pl.num_programs(2) - 1)
    def _():
        d_ref[...] = (p_acc[...] - q_acc[...]).astype(BF16)
        m_ref[...] = (p_acc[...] + q_acc[...]).astype(BF16)


def _fourier_positions(ctab, stab, uc, us):
    bn, lp, fw = uc.shape
    hl = ctab.shape[0]
    whole = lambda n: pl.BlockSpec((1, n, fw), lambda i: (i, 0, 0))
    half = jax.ShapeDtypeStruct((bn, hl, fw), BF16)
    e, o = pl.pallas_call(
        _fold_kernel, grid=(bn,), in_specs=[whole(lp), whole(lp)], out_specs=[whole(hl), whole(hl)],
        out_shape=[half, half], compiler_params=_cparams("parallel"), name="fourier_fold",
    )(uc, us)
    nb = 4 if bn % 4 == 0 else (2 if bn % 2 == 0 else 1)
    tm = _pick_tile(hl, 1152, 128)
    tk = _pick_tile(hl, 768, 128)
    d, m = pl.pallas_call(
        _dft_kernel,
        grid=(bn // nb, hl // tm, hl // tk),
        in_specs=[pl.BlockSpec((tm, tk), lambda g, i, k: (i, k)),
                  pl.BlockSpec((tm, tk), lambda g, i, k: (i, k)),
                  pl.BlockSpec((nb, tk, fw), lambda g, i, k: (g, k, 0)),
                  pl.BlockSpec((nb, tk, fw), lambda g, i, k: (g, k, 0))],
        out_specs=[pl.BlockSpec((nb, tm, fw), lambda g, i, k: (g, i, 0))] * 2,
        out_shape=[half, half],
        scratch_shapes=[pltpu.VMEM((nb, tm, fw), F32)] * 2,
        compiler_params=_cparams("parallel", "parallel", "arbitrary"),
        name="fourier_positions",
    )(ctab, stab, e, o)
    return pl.pallas_call(
        _unfold_kernel, grid=(bn,), in_specs=[whole(hl), whole(hl)], out_specs=whole(lp),
        out_shape=jax.ShapeDtypeStruct((bn, lp, fw), BF16), compiler_params=_cparams("parallel"),
        name="fourier_unfold",
    )(d, m)


def _conv_kernel(seq, qm_ref, qp_ref, qn_ref, gcol_ref, grow_ref, shift_ref, w_ref, ac_ref, dc_ref, ar_ref, dr_ref,
                 q_ref, k_ref, v_ref, gco_ref, gro_ref, xx_ref):
    tm = qm_ref.shape[1]
    j = pl.program_id(1)
    lp = seq + TAIL

    def pad_mask(start, n):
        p = start + lax.broadcasted_iota(jnp.int32, (n, 1), 0)
        p = jnp.where(p < 0, p + lp, jnp.where(p >= lp, p - lp, p))
        return (p < seq) | (p >= lp - N_META)

    live = pad_mask(j * tm, tm)
    zero = jnp.zeros((), BF16)
    xx_ref[:HALO, :] = jnp.where(pad_mask(j * tm - HALO, HALO), qp_ref[0], zero)
    xx_ref[HALO:HALO + tm, :] = jnp.where(live, qm_ref[0], zero)
    xx_ref[HALO + tm:2 * HALO + tm, :] = jnp.where(pad_mask(j * tm + tm, HALO), qn_ref[0], zero)
    xx_ref[2 * HALO + tm:, :] = jnp.zeros((xx_ref.shape[0] - 2 * HALO - tm, xx_ref.shape[1]), BF16)

    half = CONV_WIDTH // 2
    taps = [t for t in range(CONV_WIDTH) if t != half]
    for part, dst in enumerate((q_ref, k_ref, v_ref)):
        cols = slice(part * DW, (part + 1) * DW)
        for i in range(tm // 128):
            shifted = _dot(shift_ref[...], xx_ref[128 * i:128 * i + 256, cols])
            y = xx_ref[HALO + 128 * i:HALO + 128 * (i + 1), cols].astype(F32) * w_ref[half:half + 1, cols]
            for n, t in enumerate(taps):
                y = y + shifted[128 * n:128 * (n + 1)] * w_ref[t:t + 1, cols]
            y = jnp.where(pad_mask(j * tm + 128 * i, 128), y * _sigmoid(y), 0.0)
            rows = slice(128 * i, 128 * (i + 1))
            if part == 2:
                dst[0, rows, :] = y.astype(BF16)
                continue
            scale = HEAD_DIM ** -0.5 if part == 0 else 1.0
            for h in range(N_HEADS):
                lo = h * HEAD_DIM
                yh = y[:, lo:lo + HEAD_DIM]
                yh = yh * (lax.rsqrt(jnp.sum(yh * yh, -1, keepdims=True) + L2_EPS) * scale)
                dst[0, rows, lo:lo + HEAD_DIM] = yh.astype(BF16)

    gl = gcol_ref[0]
    beta = jnp.where(live, _sigmoid(gl[:, :NGATE]), 0.0)
    g = jnp.where(live, -jnp.exp(ac_ref[...]) * _softplus(gl[:, NGATE:] + dc_ref[...]), 0.0)
    i0 = lax.broadcasted_iota(jnp.int32, (CHUNK, CHUNK), 0)
    i1 = lax.broadcasted_iota(jnp.int32, (CHUNK, CHUNK), 1)
    tri_lo = (i0 >= i1).astype(BF16)
    tri_up = (i0 <= i1).astype(BF16)
    is_fwd_col = lax.broadcasted_iota(jnp.int32, (1, NGATE), 1) < N_HEADS
    g1 = g.astype(BF16)
    g2 = (g - g1.astype(F32)).astype(BF16)
    g3 = (g - g1.astype(F32) - g2.astype(F32)).astype(BF16)
    for c in range(tm // CHUNK):
        sl = slice(c * CHUNK, (c + 1) * CHUNK)
        pre = _dot(tri_lo, g1[sl]) + _dot(tri_lo, g2[sl]) + _dot(tri_lo, g3[sl])
        suf = _dot(tri_up, g1[sl]) + _dot(tri_up, g2[sl]) + _dot(tri_up, g3[sl])
        gco_ref[0, sl, :NGATE] = beta[sl]
        gco_ref[0, sl, NGATE:] = jnp.where(is_fwd_col, pre, suf)

    nc = grow_ref.shape[0]
    glr = grow_ref[:, NGATE:, :]
    shp = (nc, NGATE, CHUNK)
    pr = j * tm + lax.broadcasted_iota(jnp.int32, shp, 0) * CHUNK + lax.broadcasted_iota(jnp.int32, shp, 2)
    live_r = (pr < seq) | (pr >= lp - N_META)
    gr = jnp.where(live_r, -jnp.exp(ar_ref[...])[None] * _softplus(glr + dr_ref[...][None]), 0.0)
    gr = gr.reshape(nc * NGATE, CHUNK)
    r1 = gr.astype(BF16)
    r2 = (gr - r1.astype(F32)).astype(BF16)
    r3 = (gr - r1.astype(F32) - r2.astype(F32)).astype(BF16)
    pre_r = _dot(r1, tri_up) + _dot(r2, tri_up) + _dot(r3, tri_up)
    suf_r = _dot(r1, tri_lo) + _dot(r2, tri_lo) + _dot(r3, tri_lo)
    is_fwd_row = lax.broadcasted_iota(jnp.int32, shp, 1) < N_HEADS
    gro_ref[...] = jnp.where(is_fwd_row, pre_r.reshape(shp), suf_r.reshape(shp))


def _conv_and_gates(qkv, gcol, grow, conv_w, a_log, dt_bias, seq):
    bn, lp, qw = qkv.shape
    tm = _pick_tile(lp, 1408, 128)
    nt = lp // tm
    nc = tm // CHUNK
    hb = tm // HALO
    nhb = lp // HALO
    a_flat = a_log.reshape(1, NGATE).astype(F32)
    d_flat = dt_bias.reshape(1, NGATE).astype(F32)
    small = lambda shape: pl.BlockSpec(shape, lambda i, j: (0,) * len(shape))
    tok = lambda n: pl.BlockSpec((1, tm, n), lambda i, j: (i, j, 0))
    half = CONV_WIDTH // 2
    rr = jnp.arange(128, dtype=jnp.int32)[:, None]
    cc = jnp.arange(256, dtype=jnp.int32)[None, :]
    shift = jnp.concatenate([(cc == rr + HALO + t - half) for t in range(CONV_WIDTH) if t != half], 0).astype(BF16)
    return pl.pallas_call(
        functools.partial(_conv_kernel, seq),
        grid=(bn, nt),
        in_specs=[tok(qw),
                  pl.BlockSpec((1, HALO, qw), lambda i, j: (i, (j * hb + nhb - 1) % nhb, 0)),
                  pl.BlockSpec((1, HALO, qw), lambda i, j: (i, ((j + 1) * hb) % nhb, 0)),
                  tok(2 * NGATE),
                  pl.BlockSpec((nc, 2 * NGATE, CHUNK), lambda i, j: (i * nt + j, 0, 0)),
                  small(shift.shape),
                  small((CONV_WIDTH, qw)), small((1, NGATE)), small((1, NGATE)),
                  small((NGATE, 1)), small((NGATE, 1))],
        out_specs=[tok(DW), tok(DW), tok(DW), tok(2 * NGATE),
                   pl.BlockSpec((nc, NGATE, CHUNK), lambda i, j: (i * nt + j, 0, 0))],
        out_shape=[jax.ShapeDtypeStruct((bn, lp, DW), BF16)] * 3
        + [jax.ShapeDtypeStruct((bn, lp, 2 * NGATE), F32),
           jax.ShapeDtypeStruct((bn * lp // CHUNK, NGATE, CHUNK), F32)],
        scratch_shapes=[pltpu.VMEM((tm + 128, qw), BF16)],
        compiler_params=_cparams("parallel", "parallel"),
        name="conv_gates",
    )(qkv, qkv, qkv, gcol, grow, shift, conv_w.astype(F32), a_flat, d_flat, a_flat.T, d_flat.T)


class _Chain:
    pass


def _delta_kernel(qf_ref, kf_ref, vf_ref, cf_ref, rf_ref, qb_ref, kb_ref, vb_ref, cb_ref, rb_ref,
                  of_ref, ob_ref, s_ref):
    step = pl.program_id(1)
    nb = qf_ref.shape[0]

    @pl.when(step == 0)
    def _():
        s_ref[...] = jnp.zeros_like(s_ref)

    i0 = lax.broadcasted_iota(jnp.int32, (CHUNK, CHUNK), 0)
    i1 = lax.broadcasted_iota(jnp.int32, (CHUNK, CHUNK), 1)
    eye = (i0 == i1).astype(F32)
    level_masks = []
    b = 1
    while b < CHUNK:
        level_masks.append((((i0 // (2 * b)) == (i1 // (2 * b))) & ((i0 // b) != (i1 // b))).astype(F32))
        b *= 2

    dirs = ((qf_ref, kf_ref, vf_ref, cf_ref, rf_ref, of_ref), (qb_ref, kb_ref, vb_ref, cb_ref, rb_ref, ob_ref))
    chains = []
    for bi in range(nb):
        for r, (q_ref, k_ref, v_ref, c_ref, r_ref, o_ref) in enumerate(dirs):
            incl = (i0 >= i1) if r == 0 else (i0 <= i1)
            strict = (i0 > i1) if r == 0 else (i0 < i1)
            last = CHUNK - 1 if r == 0 else 0
            gcol = c_ref[bi]
            grow = r_ref[bi, 0]
            for h in range(N_HEADS):
                ch = _Chain()
                c = r * N_HEADS + h
                lo = h * HEAD_DIM
                ch.o_ref, ch.bi, ch.lo, ch.slot = o_ref, bi, lo, bi * NGATE + c
                ch.q = q_ref[bi, :, lo:lo + HEAD_DIM]
                ch.k = k_ref[bi, :, lo:lo + HEAD_DIM]
                v = v_ref[bi, :, lo:lo + HEAD_DIM].astype(F32)
                kf32 = ch.k.astype(F32)
                beta = gcol[:, c:c + 1]
                gc = gcol[:, NGATE + c:NGATE + c + 1]
                gr = grow[c:c + 1, :]
                g_last = gcol[last:last + 1, NGATE + c:NGATE + c + 1]
                e_gc = jnp.exp(gc)
                ch.decay = jnp.exp(jnp.where(incl, gc - gr, -1e30))
                ch.decay_strict = jnp.where(strict, ch.decay, 0.0)
                kbeta = kf32 * beta
                ch.kbeta = kbeta.astype(BF16)
                ch.rhs = jnp.concatenate([v * beta, kbeta * e_gc], axis=-1).astype(BF16)
                ch.q_decay = (ch.q.astype(F32) * e_gc).astype(BF16)
                ch.k_tail = (kf32 * jnp.exp(g_last - gc)).astype(BF16)
                ch.chunk_decay = jnp.exp(g_last)
                chains.append(ch)

    for ch in chains:
        both = _dot_nt(jnp.concatenate([ch.kbeta, ch.q], axis=0), ch.k)
        ch.lmat = both[:CHUNK] * ch.decay_strict
        ch.attn = (both[CHUNK:] * ch.decay).astype(BF16)
        ch.x = eye - ch.lmat * level_masks[0]
    for m in level_masks[1:]:
        for ch in chains:
            ch.xb = ch.x.astype(BF16)
            ch.t = _dot(ch.xb, (ch.lmat * m).astype(BF16)).astype(BF16)
        for ch in chains:
            ch.x = ch.x - _dot(ch.t, ch.xb)
    for ch in chains:
        ch.sol = _dot(ch.x.astype(BF16), ch.rhs)
    for ch in chains:
        ch.s_old = s_ref[ch.slot]
        ch.sb = ch.s_old.astype(BF16)
        lhs = jnp.concatenate([ch.sol[:, HEAD_DIM:].astype(BF16), ch.q_decay], axis=0)
        ch.ks = _dot(lhs, ch.sb)
    for ch in chains:
        v_new = (ch.sol[:, :HEAD_DIM] - ch.ks[:CHUNK]).astype(BF16)
        o = ch.ks[CHUNK:] + _dot(ch.attn, v_new)
        s_ref[ch.slot] = ch.s_old * ch.chunk_decay + _dot_tn(ch.k_tail, v_new)
        ch.o_ref[ch.bi, :, ch.lo:ch.lo + HEAD_DIM] = o.astype(BF16)


def _delta_rule(q, k, v, gco, gro, seq):
    bn, lp, dw = q.shape
    nch = lp // CHUNK
    nx = seq // CHUNK
    ntail = nch - nx

    def fwd(s):
        return jnp.where(s < ntail, nch - 1 - s, s - ntail)

    def bwd(s):
        return jnp.where(s < nx, nx - 1 - s, s)

    nb = DELTA_BATCHES_PER_STEP if bn % DELTA_BATCHES_PER_STEP == 0 else 1
    gro = gro.reshape(bn, nch, NGATE, CHUNK)

    def tok(order):
        return pl.BlockSpec((nb, CHUNK, dw), lambda i, s: (i, order(s), 0))

    def col(order):
        return pl.BlockSpec((nb, CHUNK, 2 * NGATE), lambda i, s: (i, order(s), 0))

    def row(order):
        return pl.BlockSpec((nb, 1, NGATE, CHUNK), lambda i, s: (i, order(s), 0, 0))

    return pl.pallas_call(
        _delta_kernel,
        grid=(bn // nb, nch),
        in_specs=[tok(fwd), tok(fwd), tok(fwd), col(fwd), row(fwd),
                  tok(bwd), tok(bwd), tok(bwd), col(bwd), row(bwd)],
        out_specs=[tok(fwd), tok(bwd)],
        out_shape=[jax.ShapeDtypeStruct((bn, lp, dw), BF16)] * 2,
        scratch_shapes=[pltpu.VMEM((nb * NGATE, HEAD_DIM, HEAD_DIM), F32)],
        compiler_params=_cparams("parallel", "arbitrary"),
        name="delta_rule",
    )(q, k, v, gco, gro, q, k, v, gco, gro)


def _post_kernel(alpha, y_ref, of_ref, ob_ref, z_ref, gate_ref, h_ref, wf_ref, wd_ref, wo_ref,
                 ng_ref, g_ref, b_ref, o_ref, dn_ref):
    d = h_ref.shape[-1]
    ng = ng_ref[...]
    for hh in range(N_HEADS):
        sl = slice(hh * HEAD_DIM, (hh + 1) * HEAD_DIM)
        o = of_ref[:, sl].astype(F32) + ob_ref[:, sl].astype(F32)
        z = z_ref[:, sl].astype(F32)
        o = o * lax.rsqrt(jnp.mean(o * o, -1, keepdims=True) + RMS_EPS) * ng * (z * _sigmoid(z))
        dn_ref[:, sl] = o.astype(BF16)
    y_a = _dot(y_ref[...], wf_ref[...])
    y_b = _dot(dn_ref[...], wd_ref[...])
    m = gate_ref[:, :d].astype(F32) * y_a + gate_ref[:, d:].astype(F32) * y_b
    mix = _dot(m.astype(BF16), wo_ref[...])
    o_ref[...] = _layer_norm(alpha * h_ref[...] + mix, g_ref[...], b_ref[...])


def _merge_and_project(alpha, y, o_f, o_b, z, gates, h, w_f, w_d, w_o, norm_g, ln_g, ln_b):
    r, d = h.shape
    tm = _pick_tile(r, 512, 128)
    row = lambda n: pl.BlockSpec((tm, n), lambda i: (i, 0))
    wf, wd, wo = w_f.astype(BF16), w_d.astype(BF16), w_o.astype(BF16)
    return pl.pallas_call(
        functools.partial(_post_kernel, alpha),
        grid=(r // tm,),
        in_specs=[row(FW), row(DW), row(DW), row(DW), row(2 * d), row(d),
                  _resident(wf.shape), _resident(wd.shape), _resident(wo.shape),
                  _resident((1, HEAD_DIM)), _resident((1, d)), _resident((1, d))],
        out_specs=row(d),
        out_shape=jax.ShapeDtypeStruct((r, d), F32),
        scratch_shapes=[pltpu.VMEM((tm, DW), BF16)],
        compiler_params=_cparams("parallel"),
        name="merge_project",
    )(y, o_f, o_b, z, gates, h, wf, wd, wo, norm_g.reshape(1, HEAD_DIM).astype(F32),
      ln_g.reshape(1, d).astype(F32), ln_b.reshape(1, d).astype(F32))


def _ffn_kernel(alpha, fchunk, x_ref, wu_ref, wd_ref, g_ref, b_ref, o_ref, acc_ref):
    x = x_ref[0]
    xb = x.astype(BF16)
    nf = wu_ref.shape[1] // fchunk
    for f in range(nf):
        sl = slice(f * fchunk, (f + 1) * fchunk)
        a = jnp.maximum(_dot(xb, wu_ref[:, sl]), 0.0)
        part = _dot((a * a).astype(BF16), wd_ref[sl, :])
        if f == 0:
            acc_ref[...] = part
        else:
            acc_ref[...] += part
    o_ref[0] = _layer_norm(alpha * x + acc_ref[...], g_ref[...], b_ref[...])


def _ffn(alpha, x, w_up, w_down, ln_g, ln_b, rows_out):
    bn, lp, d = x.shape
    dff = w_up.shape[1]
    tm = _pick_tile(rows_out, 512, 128)
    wu, wd = w_up.astype(BF16), w_down.astype(BF16)
    tok = pl.BlockSpec((1, tm, d), lambda i, j: (i, j, 0))
    return pl.pallas_call(
        functools.partial(_ffn_kernel, alpha, min(dff, 512)),
        grid=(bn, rows_out // tm),
        in_specs=[tok, _resident(wu.shape), _resident(wd.shape), _resident((1, d)), _resident((1, d))],
        out_specs=tok,
        out_shape=jax.ShapeDtypeStruct((bn, rows_out, d), F32),
        scratch_shapes=[pltpu.VMEM((tm, d), F32)],
        compiler_params=_cparams("parallel", "parallel"),
        name="ffn",
    )(x, wu, wd, ln_g.reshape(1, d).astype(F32), ln_b.reshape(1, d).astype(F32))


def kernel(x, meta_tokens, ln0_g, ln0_b, w_in, conv_w, a_log, dt_bias, delta_norm_g, w_fourier_proj,
           w_delta_proj, w_out, ln1_g, ln1_b, w_up, w_down, ln2_g, ln2_b):
    bn, seq, d = x.shape
    depth = w_in.shape[0]
    assert seq % 256 == 0 and d == 8 * GROUP_DIM
    lp = seq + TAIL
    r = bn * lp
    alpha = (2 * depth) ** 0.25

    h = _input_norm(x, meta_tokens.astype(x.dtype), ln0_g, ln0_b)
    ctab, stab = _dft_tables(seq)
    cg2, sg2 = _group_dft_mats()
    for l in range(depth):
        uc, us, qkv, gcol, grow, z, gates = _in_projection(h.reshape(r, d), w_in[l], cg2, sg2)
        y = _fourier_positions(ctab, stab, uc.reshape(bn, lp, FW), us.reshape(bn, lp, FW))
        q, k, v, gco, gro = _conv_and_gates(qkv.reshape(bn, lp, 3 * DW), gcol.reshape(bn, lp, 2 * NGATE),
                                            grow, conv_w[l], a_log[l], dt_bias[l], seq)
        o_f, o_b = _delta_rule(q, k, v, gco, gro, seq)
        x1 = _merge_and_project(alpha, y.reshape(r, FW), o_f.reshape(r, DW), o_b.reshape(r, DW), z, gates,
                                h.reshape(r, d), w_fourier_proj[l], w_delta_proj[l], w_out[l],
                                delta_norm_g[l], ln1_g[l], ln1_b[l])
        last = l == depth - 1
        if last:
            return _ffn(alpha, x1.reshape(bn, lp, d), w_up[l], w_down[l], ln2_g[l], ln2_b[l], seq)
        h = _ffn(alpha, x1.reshape(1, r, d), w_up[l], w_down[l], ln2_g[l], ln2_b[l], r)
```

```python
import functools
import math

import jax
import jax.numpy as jnp
from jax import lax
from jax.experimental import pallas as pl
from jax.experimental.pallas import tpu as pltpu

N_META = 16
N_GROUPS = 4
GROUP_DIM = 128
N_HEADS = 4
HEAD_DIM = 128
N_DIRS = 2
CONV_WIDTH = 5
CHUNK = 64
TAIL = 128
HALO = 16
LN_EPS = 1e-5
RMS_EPS = 1e-6
L2_EPS = 1e-6
FW = N_GROUPS * GROUP_DIM
DW = N_HEADS * HEAD_DIM
NGATE = N_DIRS * N_HEADS
VMEM_LIMIT = 56 * 1024 * 1024
DELTA_BATCHES_PER_STEP = 4
ROW_TILE = 1024
SUB_ROWS = 512

F32 = jnp.float32
BF16 = jnp.bfloat16


def _cparams(*sem):
    return pltpu.CompilerParams(dimension_semantics=sem, vmem_limit_bytes=VMEM_LIMIT)


def _dot(a, b):
    return jnp.dot(a, b, preferred_element_type=F32)


def _dot_nt(a, b):
    return lax.dot_general(a, b, (((1,), (1,)), ((), ())), preferred_element_type=F32)


def _dot_tn(a, b):
    return lax.dot_general(a, b, (((0,), (0,)), ((), ())), preferred_element_type=F32)


def _layer_norm(x, g, b):
    mu = jnp.mean(x, -1, keepdims=True)
    xc = x - mu
    var = jnp.mean(xc * xc, -1, keepdims=True)
    return xc * lax.rsqrt(var + LN_EPS) * g + b


def _sigmoid(x):
    return 1.0 / (1.0 + jnp.exp(-x))


def _softplus(x):
    return jnp.maximum(x, 0.0) + jnp.log1p(jnp.exp(-jnp.abs(x)))


def _pick_tile(n, target, mult):
    best = None
    for t in range(mult, min(n, target) + 1, mult):
        if n % t == 0:
            best = t
    assert best is not None, (n, target, mult)
    return best


def _resident(shape):
    nd = len(shape)
    return pl.BlockSpec(shape, lambda *_: (0,) * nd, pipeline_mode=pl.Buffered(1))


def _cast_kernel(w_ref, o_ref):
    o_ref[...] = w_ref[...].astype(BF16)


def _to_bf16(w):
    depth, rows, cols = w.shape
    tr = _pick_tile(rows, 512, 16)
    spec = pl.BlockSpec((1, tr, cols), lambda l, i: (l, i, 0))
    return pl.pallas_call(
        _cast_kernel, grid=(depth, rows // tr), in_specs=[spec], out_specs=spec,
        out_shape=jax.ShapeDtypeStruct(w.shape, BF16), compiler_params=_cparams("parallel", "parallel"),
        name="weights_to_bf16",
    )(w)


def _ln0_kernel(x_ref, meta_ref, g_ref, b_ref, o_ref):
    j = pl.program_id(1)
    t0, d = x_ref.shape[1], x_ref.shape[2]
    seq = o_ref.shape[1] - TAIL
    o_ref[0, pl.ds(pl.multiple_of(j * t0, t0), t0), :] = _layer_norm(x_ref[0], g_ref[...], b_ref[...])

    @pl.when(j == 0)
    def _():
        o_ref[0, seq:seq + TAIL - N_META, :] = jnp.zeros((TAIL - N_META, d), F32)
        o_ref[0, seq + TAIL - N_META:, :] = _layer_norm(meta_ref[...], g_ref[...], b_ref[...])


def _input_norm(x, meta, g, b):
    bn, seq, d = x.shape
    lp = seq + TAIL
    t0 = _pick_tile(seq, 1024, 128)
    return pl.pallas_call(
        _ln0_kernel,
        grid=(bn, seq // t0),
        in_specs=[pl.BlockSpec((1, t0, d), lambda i, j: (i, j, 0)),
                  pl.BlockSpec((N_META, d), lambda i, j: (0, 0)),
                  pl.BlockSpec((1, d), lambda i, j: (0, 0)),
                  pl.BlockSpec((1, d), lambda i, j: (0, 0))],
        out_specs=pl.BlockSpec((1, lp, d), lambda i, j: (i, 0, 0)),
        out_shape=jax.ShapeDtypeStruct((bn, lp, d), F32),
        compiler_params=_cparams("parallel", "arbitrary"),
        name="ln0",
    )(x, meta, g.reshape(1, d), b.reshape(1, d))


def _inproj_kernel(h_ref, wu_ref, wqkv_ref, wgc_ref, wgr_ref, wz_ref, wg_ref, cg_ref, sg_ref,
                   uc_ref, us_ref, qkv_ref, gcol_ref, grow_ref, z_ref, gate_ref):
    x = h_ref[...].astype(BF16)
    u = _dot(x, wu_ref[...]).astype(BF16)
    half = 2 * GROUP_DIM
    for i in range(FW // half):
        ui = u[:, i * half:(i + 1) * half]
        uc_ref[:, i * half:(i + 1) * half] = _dot(ui, cg_ref[...]).astype(BF16)
        us_ref[:, i * half:(i + 1) * half] = _dot(ui, sg_ref[...]).astype(BF16)
    qkv_ref[...] = _dot(x, wqkv_ref[...]).astype(BF16)
    gcol_ref[...] = _dot(x, wgc_ref[...])
    gr = _dot_nt(wgr_ref[...], x)
    for c in range(grow_ref.shape[0]):
        grow_ref[c] = gr[:, c * CHUNK:(c + 1) * CHUNK]
    z_ref[...] = _dot(x, wz_ref[...]).astype(BF16)
    gate_ref[...] = _sigmoid(_dot(x, wg_ref[...])).astype(BF16)


def _in_projection(h, wb, cg2, sg2):
    r, d = h.shape
    tm = _pick_tile(r, ROW_TILE, 128)
    qw = 3 * DW
    o0, o1, o2, o3 = FW, FW + qw, FW + qw + 2 * NGATE, FW + qw + 2 * NGATE + DW
    wu, wqkv, wgc, wz, wg = wb[:, :o0], wb[:, o0:o1], wb[:, o1:o2], wb[:, o2:o3], wb[:, o3:]
    wgr = wgc.T
    gw = wg.shape[1]
    row = lambda n: pl.BlockSpec((tm, n), lambda i: (i, 0))
    outs = pl.pallas_call(
        _inproj_kernel,
        grid=(r // tm,),
        in_specs=[row(d), _resident(wu.shape), _resident(wqkv.shape), _resident(wgc.shape),
                  _resident(wgr.shape), _resident(wz.shape), _resident(wg.shape),
                  _resident(cg2.shape), _resident(sg2.shape)],
        out_specs=[row(FW), row(FW), row(qw), row(2 * NGATE),
                   pl.BlockSpec((tm // CHUNK, 2 * NGATE, CHUNK), lambda i: (i, 0, 0)),
                   row(DW), row(gw)],
        out_shape=[jax.ShapeDtypeStruct((r, FW), BF16), jax.ShapeDtypeStruct((r, FW), BF16),
                   jax.ShapeDtypeStruct((r, qw), BF16), jax.ShapeDtypeStruct((r, 2 * NGATE), F32),
                   jax.ShapeDtypeStruct((r // CHUNK, 2 * NGATE, CHUNK), F32),
                   jax.ShapeDtypeStruct((r, DW), BF16), jax.ShapeDtypeStruct((r, gw), BF16)],
        compiler_params=_cparams("parallel"),
        name="in_projection",
    )(h, wu, wqkv, wgc, wgr, wz, wg, cg2, sg2)
    return outs


def _half_rows(seq):
    return seq // 2 + 256


def _dft_tables(seq):
    n = seq + N_META
    hl = _half_rows(seq)
    nblk = hl // 128
    nxb = seq // 256
    m = jnp.arange(hl, dtype=jnp.int32)
    self_row = seq // 2 - N_META // 2
    pos = jnp.where(m < seq // 2, m + N_META, m - seq // 2)
    valid = (m <= self_row) | ((m >= seq // 2) & (m < seq // 2 + N_META))
    weight = jnp.where(valid, jnp.where(m == self_row, 0.5, 1.0), 0.0).astype(F32)
    pk = jnp.where(valid, pos, 0)
    blk = jnp.arange(nblk, dtype=jnp.int32)
    base = jnp.where(blk < nxb, N_META + 128 * blk, 0)
    w = 2.0 * math.pi / n
    pa = ((pk[:, None] * base[None, :]) % n).astype(F32) * w
    pb = ((pk[:, None] * jnp.arange(128, dtype=jnp.int32)[None, :]) % n).astype(F32) * w
    ca, sa, cb, sb = jnp.cos(pa), jnp.sin(pa), jnp.cos(pb), jnp.sin(pb)
    scale = (weight[:, None] * weight[None, :]) * n ** -0.5
    c = (ca[:, :, None] * cb[:, None, :] - sa[:, :, None] * sb[:, None, :]).reshape(hl, hl) * scale
    s = (sa[:, :, None] * cb[:, None, :] + ca[:, :, None] * sb[:, None, :]).reshape(hl, hl) * scale
    return c.astype(BF16), s.astype(BF16)


def _group_dft_mats():
    i = jnp.arange(GROUP_DIM, dtype=jnp.int32)
    ang = ((i[:, None] * i[None, :]) % GROUP_DIM).astype(F32) * (2.0 * math.pi / GROUP_DIM)
    eye2 = jnp.eye(2, dtype=F32)
    cg = jnp.kron(eye2, jnp.cos(ang)) * GROUP_DIM ** -0.5
    sg = jnp.kron(eye2, jnp.sin(ang)) * GROUP_DIM ** -0.5
    return cg.astype(BF16), sg.astype(BF16)


def _mirror_select():
    r = lax.broadcasted_iota(jnp.int32, (128, 256), 0)
    c = lax.broadcasted_iota(jnp.int32, (128, 256), 1)
    rm = lax.broadcasted_iota(jnp.int32, (N_META, 128), 0)
    cm = lax.broadcasted_iota(jnp.int32, (N_META, 128), 1)
    return (c == 240 - r).astype(BF16), ((cm == 128 - rm) & (rm > 0)).astype(BF16)


def _fold_kernel(a_ref, b_ref, e_ref, o_ref):
    lp = a_ref.shape[1]
    seq = lp - TAIL
    sel, sel_meta = _mirror_select()
    for src, dst, sign in ((a_ref, e_ref, 1.0), (b_ref, o_ref, -1.0)):
        for i in range(seq // 256):
            win = seq - 256 - 128 * i
            rev = _dot(sel, src[0, win:win + 256, :])
            dst[0, 128 * i:128 * (i + 1), :] = (src[0, 128 * i:128 * (i + 1), :].astype(F32) + sign * rev).astype(BF16)
        rev = _dot(sel_meta, src[0, seq - 128:seq, :])
        xh = seq // 2
        dst[0, xh:xh + N_META, :] = (src[0, lp - N_META:, :].astype(F32) + sign * rev).astype(BF16)
        dst[0, xh + N_META:, :] = jnp.zeros((dst.shape[1] - xh - N_META, dst.shape[2]), BF16)


def _unfold_kernel(d_ref, s_ref, y_ref):
    lp = y_ref.shape[1]
    seq = lp - TAIL
    fw = y_ref.shape[2]
    nxb = seq // 256
    xh = seq // 2
    sel, sel_meta = _mirror_select()
    for j in range(seq // 128):
        y = d_ref[0, 128 * j:128 * (j + 1), :].astype(F32) if j < nxb else None
        i1 = seq // 128 - 2 - j
        ok1, ok2 = 0 <= i1 < nxb, 0 <= i1 + 1 < nxb
        if ok1 and ok2:
            hi = _dot(sel, s_ref[0, 128 * i1:128 * i1 + 256, :])
        elif ok1:
            hi = _dot(sel[:, :128], s_ref[0, 128 * i1:128 * i1 + 128, :])
        elif ok2:
            hi = _dot(sel[:, 128:], s_ref[0, 128 * (i1 + 1):128 * (i1 + 2), :])
        else:
            hi = None
        if j == seq // 128 - 1:
            hi = hi + _dot_tn(sel_meta, s_ref[0, xh:xh + N_META, :])
        y = hi if y is None else (y if hi is None else y + hi)
        y_ref[0, 128 * j:128 * (j + 1), :] = y.astype(BF16)
    y_ref[0, seq:lp - N_META, :] = jnp.zeros((TAIL - N_META, fw), BF16)
    y_ref[0, lp - N_META:, :] = d_ref[0, xh:xh + N_META, :]


def _dft_kernel(c_ref, s_ref, e_ref, o_ref, d_ref, m_ref, p_acc, q_acc):
    kt = pl.program_id(2)

    @pl.when(kt == 0)
    def _():
        p_acc[...] = jnp.zeros_like(p_acc)
        q_acc[...] = jnp.zeros_like(q_acc)

    c = c_ref[...]
    s = s_ref[...]
    for i in range(e_ref.shape[0]):
        p_acc[i] += _dot(c, e_ref[i])
        q_acc[i] += _dot(s, o_ref[i])

    @pl.when(kt == pl.num_programs(2) - 1)
    def _():
        d_ref[...] = (p_acc[...] - q_acc[...]).astype(BF16)
        m_ref[...] = (p_acc[...] + q_acc[...]).astype(BF16)


def _fourier_positions(ctab, stab, uc, us):
    bn, lp, fw = uc.shape
    hl = ctab.shape[0]
    whole = lambda n: pl.BlockSpec((1, n, fw), lambda i: (i, 0, 0))
    half = jax.ShapeDtypeStruct((bn, hl, fw), BF16)
    e, o = pl.pallas_call(
        _fold_kernel, grid=(bn,), in_specs=[whole(lp), whole(lp)], out_specs=[whole(hl), whole(hl)],
        out_shape=[half, half], compiler_params=_cparams("parallel"), name="fourier_fold",
    )(uc, us)
    nb = 4 if bn % 4 == 0 else (2 if bn % 2 == 0 else 1)
    tm = _pick_tile(hl, 1152, 128)
    tk = _pick_tile(hl, 768, 128)
    d, m = pl.pallas_call(
        _dft_kernel,
        grid=(bn // nb, hl // tm, hl // tk),
        in_specs=[pl.BlockSpec((tm, tk), lambda g, i, k: (i, k)),
                  pl.BlockSpec((tm, tk), lambda g, i, k: (i, k)),
                  pl.BlockSpec((nb, tk, fw), lambda g, i, k: (g, k, 0)),
                  pl.BlockSpec((nb, tk, fw), lambda g, i, k: (g, k, 0))],
        out_specs=[pl.BlockSpec((nb, tm, fw), lambda g, i, k: (g, i, 0))] * 2,
        out_shape=[half, half],
        scratch_shapes=[pltpu.VMEM((nb, tm, fw), F32)] * 2,
        compiler_params=_cparams("parallel", "parallel", "arbitrary"),
        name="fourier_positions",
    )(ctab, stab, e, o)
    return pl.pallas_call(
        _unfold_kernel, grid=(bn,), in_specs=[whole(hl), whole(hl)], out_specs=whole(lp),
        out_shape=jax.ShapeDtypeStruct((bn, lp, fw), BF16), compiler_params=_cparams("parallel"),
        name="fourier_unfold",
    )(d, m)


def _conv_kernel(seq, qm_ref, qp_ref, qn_ref, gcol_ref, grow_ref, shift_ref, w_ref, ac_ref, dc_ref, ar_ref, dr_ref,
                 q_ref, k_ref, v_ref, gco_ref, gro_ref, xx_ref):
    tm = qm_ref.shape[1]
    j = pl.program_id(1)
    lp = seq + TAIL

    def pad_mask(start, n):
        p = start + lax.broadcasted_iota(jnp.int32, (n, 1), 0)
        p = jnp.where(p < 0, p + lp, jnp.where(p >= lp, p - lp, p))
        return (p < seq) | (p >= lp - N_META)

    live = pad_mask(j * tm, tm)
    zero = jnp.zeros((), BF16)
    xx_ref[:HALO, :] = jnp.where(pad_mask(j * tm - HALO, HALO), qp_ref[0], zero)
    xx_ref[HALO:HALO + tm, :] = jnp.where(live, qm_ref[0], zero)
    xx_ref[HALO + tm:2 * HALO + tm, :] = jnp.where(pad_mask(j * tm + tm, HALO), qn_ref[0], zero)
    xx_ref[2 * HALO + tm:, :] = jnp.zeros((xx_ref.shape[0] - 2 * HALO - tm, xx_ref.shape[1]), BF16)

    half = CONV_WIDTH // 2
    taps = [t for t in range(CONV_WIDTH) if t != half]
    for part, dst in enumerate((q_ref, k_ref, v_ref)):
        cols = slice(part * DW, (part + 1) * DW)
        for i in range(tm // 128):
            shifted = _dot(shift_ref[...], xx_ref[128 * i:128 * i + 256, cols])
            y = xx_ref[HALO + 128 * i:HALO + 128 * (i + 1), cols].astype(F32) * w_ref[half:half + 1, cols]
            for n, t in enumerate(taps):
                y = y + shifted[128 * n:128 * (n + 1)] * w_ref[t:t + 1, cols]
            y = jnp.where(pad_mask(j * tm + 128 * i, 128), y * _sigmoid(y), 0.0)
            rows = slice(128 * i, 128 * (i + 1))
            if part == 2:
                dst[0, rows, :] = y.astype(BF16)
                continue
            scale = HEAD_DIM ** -0.5 if part == 0 else 1.0
            for h in range(N_HEADS):
                lo = h * HEAD_DIM
                yh = y[:, lo:lo + HEAD_DIM]
                yh = yh * (lax.rsqrt(jnp.sum(yh * yh, -1, keepdims=True) + L2_EPS) * scale)
                dst[0, rows, lo:lo + HEAD_DIM] = yh.astype(BF16)

    gl = gcol_ref[0]
    beta = jnp.where(live, _sigmoid(gl[:, :NGATE]), 0.0)
    g = jnp.where(live, -jnp.exp(ac_ref[...]) * _softplus(gl[:, NGATE:] + dc_ref[...]), 0.0)
    i0 = lax.broadcasted_iota(jnp.int32, (CHUNK, CHUNK), 0)
    i1 = lax.broadcasted_iota(jnp.int32, (CHUNK, CHUNK), 1)
    tri_lo = (i0 >= i1).astype(BF16)
    tri_up = (i0 <= i1).astype(BF16)
    is_fwd_col = lax.broadcasted_iota(jnp.int32, (1, NGATE), 1) < N_HEADS
    g1 = g.astype(BF16)
    g2 = (g - g1.astype(F32)).astype(BF16)
    g3 = (g - g1.astype(F32) - g2.astype(F32)).astype(BF16)
    for c in range(tm // CHUNK):
        sl = slice(c * CHUNK, (c + 1) * CHUNK)
        pre = _dot(tri_lo, g1[sl]) + _dot(tri_lo, g2[sl]) + _dot(tri_lo, g3[sl])
        suf = _dot(tri_up, g1[sl]) + _dot(tri_up, g2[sl]) + _dot(tri_up, g3[sl])
        gco_ref[0, sl, :NGATE] = beta[sl]
        gco_ref[0, sl, NGATE:] = jnp.where(is_fwd_col, pre, suf)

    nc = grow_ref.shape[0]
    glr = grow_ref[:, NGATE:, :]
    shp = (nc, NGATE, CHUNK)
    pr = j * tm + lax.broadcasted_iota(jnp.int32, shp, 0) * CHUNK + lax.broadcasted_iota(jnp.int32, shp, 2)
    live_r = (pr < seq) | (pr >= lp - N_META)
    gr = jnp.where(live_r, -jnp.exp(ar_ref[...])[None] * _softplus(glr + dr_ref[...][None]), 0.0)
    gr = gr.reshape(nc * NGATE, CHUNK)
    r1 = gr.astype(BF16)
    r2 = (gr - r1.astype(F32)).astype(BF16)
    r3 = (gr - r1.astype(F32) - r2.astype(F32)).astype(BF16)
    pre_r = _dot(r1, tri_up) + _dot(r2, tri_up) + _dot(r3, tri_up)
    suf_r = _dot(r1, tri_lo) + _dot(r2, tri_lo) + _dot(r3, tri_lo)
    is_fwd_row = lax.broadcasted_iota(jnp.int32, shp, 1) < N_HEADS
    gro_ref[...] = jnp.where(is_fwd_row, pre_r.reshape(shp), suf_r.reshape(shp))


def _conv_and_gates(qkv, gcol, grow, conv_w, a_log, dt_bias, seq):
    bn, lp, qw = qkv.shape
    tm = _pick_tile(lp, 1408, 128)
    nt = lp // tm
    nc = tm // CHUNK
    hb = tm // HALO
    nhb = lp // HALO
    a_flat = a_log.reshape(1, NGATE).astype(F32)
    d_flat = dt_bias.reshape(1, NGATE).astype(F32)
    small = lambda shape: pl.BlockSpec(shape, lambda i, j: (0,) * len(shape))
    tok = lambda n: pl.BlockSpec((1, tm, n), lambda i, j: (i, j, 0))
    half = CONV_WIDTH // 2
    rr = jnp.arange(128, dtype=jnp.int32)[:, None]
    cc = jnp.arange(256, dtype=jnp.int32)[None, :]
    shift = jnp.concatenate([(cc == rr + HALO + t - half) for t in range(CONV_WIDTH) if t != half], 0).astype(BF16)
    return pl.pallas_call(
        functools.partial(_conv_kernel, seq),
        grid=(bn, nt),
        in_specs=[tok(qw),
                  pl.BlockSpec((1, HALO, qw), lambda i, j: (i, (j * hb + nhb - 1) % nhb, 0)),
                  pl.BlockSpec((1, HALO, qw), lambda i, j: (i, ((j + 1) * hb) % nhb, 0)),
                  tok(2 * NGATE),
                  pl.BlockSpec((nc, 2 * NGATE, CHUNK), lambda i, j: (i * nt + j, 0, 0)),
                  small(shift.shape),
                  small((CONV_WIDTH, qw)), small((1, NGATE)), small((1, NGATE)),
                  small((NGATE, 1)), small((NGATE, 1))],
        out_specs=[tok(DW), tok(DW), tok(DW), tok(2 * NGATE),
                   pl.BlockSpec((nc, NGATE, CHUNK), lambda i, j: (i * nt + j, 0, 0))],
        out_shape=[jax.ShapeDtypeStruct((bn, lp, DW), BF16)] * 3
        + [jax.ShapeDtypeStruct((bn, lp, 2 * NGATE), F32),
           jax.ShapeDtypeStruct((bn * lp // CHUNK, NGATE, CHUNK), F32)],
        scratch_shapes=[pltpu.VMEM((tm + 128, qw), BF16)],
        compiler_params=_cparams("parallel", "parallel"),
        name="conv_gates",
    )(qkv, qkv, qkv, gcol, grow, shift, conv_w.astype(F32), a_flat, d_flat, a_flat.T, d_flat.T)


class _Chain:
    pass


def _delta_kernel(qf_ref, kf_ref, vf_ref, cf_ref, rf_ref, qb_ref, kb_ref, vb_ref, cb_ref, rb_ref,
                  of_ref, ob_ref, s_ref):
    step = pl.program_id(1)
    nb = qf_ref.shape[0]

    @pl.when(step == 0)
    def _():
        s_ref[...] = jnp.zeros_like(s_ref)

    i0 = lax.broadcasted_iota(jnp.int32, (CHUNK, CHUNK), 0)
    i1 = lax.broadcasted_iota(jnp.int32, (CHUNK, CHUNK), 1)
    eye = (i0 == i1).astype(F32)
    level_masks = []
    b = 1
    while b < CHUNK:
        level_masks.append((((i0 // (2 * b)) == (i1 // (2 * b))) & ((i0 // b) != (i1 // b))).astype(F32))
        b *= 2

    dirs = ((qf_ref, kf_ref, vf_ref, cf_ref, rf_ref, of_ref), (qb_ref, kb_ref, vb_ref, cb_ref, rb_ref, ob_ref))
    chains = []
    for bi in range(nb):
        for r, (q_ref, k_ref, v_ref, c_ref, r_ref, o_ref) in enumerate(dirs):
            incl = (i0 >= i1) if r == 0 else (i0 <= i1)
            strict = (i0 > i1) if r == 0 else (i0 < i1)
            last = CHUNK - 1 if r == 0 else 0
            gcol = c_ref[bi]
            grow = r_ref[bi, 0]
            for h in range(N_HEADS):
                ch = _Chain()
                c = r * N_HEADS + h
                lo = h * HEAD_DIM
                ch.o_ref, ch.bi, ch.lo, ch.slot = o_ref, bi, lo, bi * NGATE + c
                ch.q = q_ref[bi, :, lo:lo + HEAD_DIM]
                ch.k = k_ref[bi, :, lo:lo + HEAD_DIM]
                v = v_ref[bi, :, lo:lo + HEAD_DIM].astype(F32)
                kf32 = ch.k.astype(F32)
                beta = gcol[:, c:c + 1]
                gc = gcol[:, NGATE + c:NGATE + c + 1]
                gr = grow[c:c + 1, :]
                g_last = gcol[last:last + 1, NGATE + c:NGATE + c + 1]
                e_gc = jnp.exp(gc)
                ch.decay = jnp.exp(jnp.where(incl, gc - gr, -1e30))
                ch.decay_strict = jnp.where(strict, ch.decay, 0.0)
                kbeta = kf32 * beta
                ch.kbeta = kbeta.astype(BF16)
                ch.rhs = jnp.concatenate([v * beta, kbeta * e_gc], axis=-1).astype(BF16)
                ch.q_decay = (ch.q.astype(F32) * e_gc).astype(BF16)
                ch.k_tail = (kf32 * jnp.exp(g_last - gc)).astype(BF16)
                ch.chunk_decay = jnp.exp(g_last)
                chains.append(ch)

    for ch in chains:
        both = _dot_nt(jnp.concatenate([ch.kbeta, ch.q], axis=0), ch.k)
        ch.lmat = both[:CHUNK] * ch.decay_strict
        ch.attn = (both[CHUNK:] * ch.decay).astype(BF16)
        ch.x = eye - ch.lmat * level_masks[0]
    for m in level_masks[1:]:
        for ch in chains:
            ch.xb = ch.x.astype(BF16)
            ch.t = _dot(ch.xb, (ch.lmat * m).astype(BF16)).astype(BF16)
        for ch in chains:
            ch.x = ch.x - _dot(ch.t, ch.xb)
    for ch in chains:
        ch.sol = _dot(ch.x.astype(BF16), ch.rhs)
    for ch in chains:
        ch.s_old = s_ref[ch.slot]
        ch.sb = ch.s_old.astype(BF16)
        lhs = jnp.concatenate([ch.sol[:, HEAD_DIM:].astype(BF16), ch.q_decay], axis=0)
        ch.ks = _dot(lhs, ch.sb)
    for ch in chains:
        v_new = (ch.sol[:, :HEAD_DIM] - ch.ks[:CHUNK]).astype(BF16)
        o = ch.ks[CHUNK:] + _dot(ch.attn, v_new)
        s_ref[ch.slot] = ch.s_old * ch.chunk_decay + _dot_tn(ch.k_tail, v_new)
        ch.o_ref[ch.bi, :, ch.lo:ch.lo + HEAD_DIM] = o.astype(BF16)


def _delta_rule(q, k, v, gco, gro, seq):
    bn, lp, dw = q.shape
    nch = lp // CHUNK
    nx = seq // CHUNK
    ntail = nch - nx

    def fwd(s):
        return jnp.where(s < ntail, nch - 1 - s, s - ntail)

    def bwd(s):
        return jnp.where(s < nx, nx - 1 - s, s)

    nb = DELTA_BATCHES_PER_STEP if bn % DELTA_BATCHES_PER_STEP == 0 else 1
    gro = gro.reshape(bn, nch, NGATE, CHUNK)

    def tok(order):
        return pl.BlockSpec((nb, CHUNK, dw), lambda i, s: (i, order(s), 0))

    def col(order):
        return pl.BlockSpec((nb, CHUNK, 2 * NGATE), lambda i, s: (i, order(s), 0))

    def row(order):
        return pl.BlockSpec((nb, 1, NGATE, CHUNK), lambda i, s: (i, order(s), 0, 0))

    return pl.pallas_call(
        _delta_kernel,
        grid=(bn // nb, nch),
        in_specs=[tok(fwd), tok(fwd), tok(fwd), col(fwd), row(fwd),
                  tok(bwd), tok(bwd), tok(bwd), col(bwd), row(bwd)],
        out_specs=[tok(fwd), tok(bwd)],
        out_shape=[jax.ShapeDtypeStruct((bn, lp, dw), BF16)] * 2,
        scratch_shapes=[pltpu.VMEM((nb * NGATE, HEAD_DIM, HEAD_DIM), F32)],
        compiler_params=_cparams("parallel", "arbitrary"),
        name="delta_rule",
    )(q, k, v, gco, gro, q, k, v, gco, gro)


def _post_kernel(alpha, y_ref, of_ref, ob_ref, z_ref, gate_ref, h_ref, wf_ref, wd_ref, wo_ref,
                 ng_ref, g_ref, b_ref, o_ref):
    d = h_ref.shape[-1]
    tm = h_ref.shape[0]
    sub = _pick_tile(tm, SUB_ROWS, 128)
    ng = ng_ref[...]
    for s in range(tm // sub):
        rows = slice(s * sub, (s + 1) * sub)
        heads = []
        for hh in range(N_HEADS):
            sl = slice(hh * HEAD_DIM, (hh + 1) * HEAD_DIM)
            o = of_ref[rows, sl].astype(F32) + ob_ref[rows, sl].astype(F32)
            z = z_ref[rows, sl].astype(F32)
            o = o * lax.rsqrt(jnp.mean(o * o, -1, keepdims=True) + RMS_EPS) * ng * (z * _sigmoid(z))
            heads.append(o.astype(BF16))
        y_a = _dot(y_ref[rows, :], wf_ref[...])
        y_b = _dot(jnp.concatenate(heads, axis=-1), wd_ref[...])
        m = gate_ref[rows, :d].astype(F32) * y_a + gate_ref[rows, d:].astype(F32) * y_b
        mix = _dot(m.astype(BF16), wo_ref[...])
        o_ref[rows, :] = _layer_norm(alpha * h_ref[rows, :] + mix, g_ref[...], b_ref[...])


def _merge_and_project(alpha, y, o_f, o_b, z, gates, h, wf, wd, wo, norm_g, ln_g, ln_b):
    r, d = h.shape
    tm = _pick_tile(r, ROW_TILE, 128)
    row = lambda n: pl.BlockSpec((tm, n), lambda i: (i, 0))
    return pl.pallas_call(
        functools.partial(_post_kernel, alpha),
        grid=(r // tm,),
        in_specs=[row(FW), row(DW), row(DW), row(DW), row(2 * d), row(d),
                  _resident(wf.shape), _resident(wd.shape), _resident(wo.shape),
                  _resident((1, HEAD_DIM)), _resident((1, d)), _resident((1, d))],
        out_specs=row(d),
        out_shape=jax.ShapeDtypeStruct((r, d), F32),
        compiler_params=_cparams("parallel"),
        name="merge_project",
    )(y, o_f, o_b, z, gates, h, wf, wd, wo, norm_g.reshape(1, HEAD_DIM).astype(F32),
      ln_g.reshape(1, d).astype(F32), ln_b.reshape(1, d).astype(F32))


def _ffn_kernel(alpha, fchunk, x_ref, wu_ref, wd_ref, g_ref, b_ref, o_ref, acc_ref):
    x = x_ref[0]
    xb = x.astype(BF16)
    nf = wu_ref.shape[1] // fchunk
    for f in range(nf):
        sl = slice(f * fchunk, (f + 1) * fchunk)
        a = jnp.maximum(_dot(xb, wu_ref[:, sl]), 0.0)
        part = _dot((a * a).astype(BF16), wd_ref[sl, :])
        if f == 0:
            acc_ref[...] = part
        else:
            acc_ref[...] += part
    o_ref[0] = _layer_norm(alpha * x + acc_ref[...], g_ref[...], b_ref[...])


def _ffn(alpha, x, wu, wd, ln_g, ln_b, rows_out):
    bn, lp, d = x.shape
    dff = wu.shape[1]
    tm = _pick_tile(rows_out, ROW_TILE, 128)
    tok = pl.BlockSpec((1, tm, d), lambda i, j: (i, j, 0))
    return pl.pallas_call(
        functools.partial(_ffn_kernel, alpha, min(dff, 512)),
        grid=(bn, rows_out // tm),
        in_specs=[tok, _resident(wu.shape), _resident(wd.shape), _resident((1, d)), _resident((1, d))],
        out_specs=tok,
        out_shape=jax.ShapeDtypeStruct((bn, rows_out, d), F32),
        scratch_shapes=[pltpu.VMEM((tm, d), F32)],
        compiler_params=_cparams("parallel", "parallel"),
        name="ffn",
    )(x, wu, wd, ln_g.reshape(1, d).astype(F32), ln_b.reshape(1, d).astype(F32))


def kernel(x, meta_tokens, ln0_g, ln0_b, w_in, conv_w, a_log, dt_bias, delta_norm_g, w_fourier_proj,
           w_delta_proj, w_out, ln1_g, ln1_b, w_up, w_down, ln2_g, ln2_b):
    bn, seq, d = x.shape
    depth = w_in.shape[0]
    assert seq % 256 == 0 and d == 8 * GROUP_DIM
    lp = seq + TAIL
    r = bn * lp
    alpha = (2 * depth) ** 0.25

    h = _input_norm(x, meta_tokens.astype(x.dtype), ln0_g, ln0_b)
    ctab, stab = _dft_tables(seq)
    cg2, sg2 = _group_dft_mats()
    w_in, w_fourier_proj, w_delta_proj, w_out, w_up, w_down = [
        _to_bf16(w) for w in (w_in, w_fourier_proj, w_delta_proj, w_out, w_up, w_down)]
    for l in range(depth):
        uc, us, qkv, gcol, grow, z, gates = _in_projection(h.reshape(r, d), w_in[l], cg2, sg2)
        y = _fourier_positions(ctab, stab, uc.reshape(bn, lp, FW), us.reshape(bn, lp, FW))
        q, k, v, gco, gro = _conv_and_gates(qkv.reshape(bn, lp, 3 * DW), gcol.reshape(bn, lp, 2 * NGATE),
                                            grow, conv_w[l], a_log[l], dt_bias[l], seq)
        o_f, o_b = _delta_rule(q, k, v, gco, gro, seq)
        x1 = _merge_and_project(alpha, y.reshape(r, FW), o_f.reshape(r, DW), o_b.reshape(r, DW), z, gates,
                                h.reshape(r, d), w_fourier_proj[l], w_delta_proj[l], w_out[l],
                                delta_norm_g[l], ln1_g[l], ln1_b[l])
        last = l == depth - 1
        if last:
            return _ffn(alpha, x1.reshape(bn, lp, d), w_up[l], w_down[l], ln2_g[l], ln2_b[l], seq)
        h = _ffn(alpha, x1.reshape(1, r, d), w_up[l], w_down[l], ln2_g[l], ln2_b[l], r)
```

```python
import functools
import math

import jax
import jax.numpy as jnp
from jax import lax
from jax.experimental import pallas as pl
from jax.experimental.pallas import tpu as pltpu

N_META = 16
N_GROUPS = 4
GROUP_DIM = 128
N_HEADS = 4
HEAD_DIM = 128
N_DIRS = 2
CONV_WIDTH = 5
CHUNK = 64
TAIL = 128
HALO = 16
LN_EPS = 1e-5
RMS_EPS = 1e-6
L2_EPS = 1e-6
FW = N_GROUPS * GROUP_DIM
DW = N_HEADS * HEAD_DIM
NGATE = N_DIRS * N_HEADS
VMEM_LIMIT = 56 * 1024 * 1024
DELTA_BATCHES_PER_STEP = 4
ROW_TILE = 1024
SUB_ROWS = 256

F32 = jnp.float32
BF16 = jnp.bfloat16


def _cparams(*sem):
    return pltpu.CompilerParams(dimension_semantics=sem, vmem_limit_bytes=VMEM_LIMIT)


def _dot(a, b):
    return jnp.dot(a, b, preferred_element_type=F32)


def _dot_nt(a, b):
    return lax.dot_general(a, b, (((1,), (1,)), ((), ())), preferred_element_type=F32)


def _dot_tn(a, b):
    return lax.dot_general(a, b, (((0,), (0,)), ((), ())), preferred_element_type=F32)


def _layer_norm(x, g, b):
    mu = jnp.mean(x, -1, keepdims=True)
    xc = x - mu
    var = jnp.mean(xc * xc, -1, keepdims=True)
    return xc * lax.rsqrt(var + LN_EPS) * g + b


def _sigmoid(x):
    return 1.0 / (1.0 + jnp.exp(-x))


def _softplus(x):
    return jnp.maximum(x, 0.0) + jnp.log1p(jnp.exp(-jnp.abs(x)))


def _pick_tile(n, target, mult):
    best = None
    for t in range(mult, min(n, target) + 1, mult):
        if n % t == 0:
            best = t
    assert best is not None, (n, target, mult)
    return best


def _resident(shape):
    nd = len(shape)
    return pl.BlockSpec(shape, lambda *_: (0,) * nd, pipeline_mode=pl.Buffered(1))


def _ln0_kernel(x_ref, meta_ref, g_ref, b_ref, o_ref):
    j = pl.program_id(1)
    t0, d = x_ref.shape[1], x_ref.shape[2]
    seq = o_ref.shape[1] - TAIL
    o_ref[0, pl.ds(pl.multiple_of(j * t0, t0), t0), :] = _layer_norm(x_ref[0], g_ref[...], b_ref[...])

    @pl.when(j == 0)
    def _():
        o_ref[0, seq:seq + TAIL - N_META, :] = jnp.zeros((TAIL - N_META, d), F32)
        o_ref[0, seq + TAIL - N_META:, :] = _layer_norm(meta_ref[...], g_ref[...], b_ref[...])


def _input_norm(x, meta, g, b):
    bn, seq, d = x.shape
    lp = seq + TAIL
    t0 = _pick_tile(seq, 1024, 128)
    return pl.pallas_call(
        _ln0_kernel,
        grid=(bn, seq // t0),
        in_specs=[pl.BlockSpec((1, t0, d), lambda i, j: (i, j, 0)),
                  pl.BlockSpec((N_META, d), lambda i, j: (0, 0)),
                  pl.BlockSpec((1, d), lambda i, j: (0, 0)),
                  pl.BlockSpec((1, d), lambda i, j: (0, 0))],
        out_specs=pl.BlockSpec((1, lp, d), lambda i, j: (i, 0, 0)),
        out_shape=jax.ShapeDtypeStruct((bn, lp, d), F32),
        compiler_params=_cparams("parallel", "arbitrary"),
        name="ln0",
    )(x, meta, g.reshape(1, d), b.reshape(1, d))


def _inproj_kernel(h_ref, wu_ref, wqkv_ref, wgc_ref, wgr_ref, wz_ref, wg_ref, cg_ref, sg_ref,
                   uc_ref, us_ref, qkv_ref, gcol_ref, grow_ref, z_ref, gate_ref):
    x = h_ref[...].astype(BF16)
    u = _dot(x, wu_ref[...]).astype(BF16)
    half = 2 * GROUP_DIM
    for i in range(FW // half):
        ui = u[:, i * half:(i + 1) * half]
        uc_ref[:, i * half:(i + 1) * half] = _dot(ui, cg_ref[...]).astype(BF16)
        us_ref[:, i * half:(i + 1) * half] = _dot(ui, sg_ref[...]).astype(BF16)
    qkv_ref[...] = _dot(x, wqkv_ref[...]).astype(BF16)
    gcol_ref[...] = _dot(x, wgc_ref[...])
    gr = _dot_nt(wgr_ref[...], x)
    for c in range(grow_ref.shape[0]):
        grow_ref[c] = gr[:, c * CHUNK:(c + 1) * CHUNK]
    z_ref[...] = _dot(x, wz_ref[...]).astype(BF16)
    gate_ref[...] = _sigmoid(_dot(x, wg_ref[...])).astype(BF16)


def _in_projection(h, wb, cg2, sg2):
    r, d = h.shape
    tm = _pick_tile(r, ROW_TILE, 128)
    qw = 3 * DW
    o0, o1, o2, o3 = FW, FW + qw, FW + qw + 2 * NGATE, FW + qw + 2 * NGATE + DW
    wu, wqkv, wgc, wz, wg = wb[:, :o0], wb[:, o0:o1], wb[:, o1:o2], wb[:, o2:o3], wb[:, o3:]
    wgr = wgc.T
    gw = wg.shape[1]
    row = lambda n: pl.BlockSpec((tm, n), lambda i: (i, 0))
    outs = pl.pallas_call(
        _inproj_kernel,
        grid=(r // tm,),
        in_specs=[row(d), _resident(wu.shape), _resident(wqkv.shape), _resident(wgc.shape),
                  _resident(wgr.shape), _resident(wz.shape), _resident(wg.shape),
                  _resident(cg2.shape), _resident(sg2.shape)],
        out_specs=[row(FW), row(FW), row(qw), row(2 * NGATE),
                   pl.BlockSpec((tm // CHUNK, 2 * NGATE, CHUNK), lambda i: (i, 0, 0)),
                   row(DW), row(gw)],
        out_shape=[jax.ShapeDtypeStruct((r, FW), BF16), jax.ShapeDtypeStruct((r, FW), BF16),
                   jax.ShapeDtypeStruct((r, qw), BF16), jax.ShapeDtypeStruct((r, 2 * NGATE), F32),
                   jax.ShapeDtypeStruct((r // CHUNK, 2 * NGATE, CHUNK), F32),
                   jax.ShapeDtypeStruct((r, DW), BF16), jax.ShapeDtypeStruct((r, gw), BF16)],
        compiler_params=_cparams("parallel"),
        name="in_projection",
    )(h, wu, wqkv, wgc, wgr, wz, wg, cg2, sg2)
    return outs


def _half_rows(seq):
    return seq // 2 + 256


def _dft_tables(seq):
    n = seq + N_META
    hl = _half_rows(seq)
    nblk = hl // 128
    nxb = seq // 256
    m = jnp.arange(hl, dtype=jnp.int32)
    self_row = seq // 2 - N_META // 2
    pos = jnp.where(m < seq // 2, m + N_META, m - seq // 2)
    valid = (m <= self_row) | ((m >= seq // 2) & (m < seq // 2 + N_META))
    weight = jnp.where(valid, jnp.where(m == self_row, 0.5, 1.0), 0.0).astype(F32)
    pk = jnp.where(valid, pos, 0)
    blk = jnp.arange(nblk, dtype=jnp.int32)
    base = jnp.where(blk < nxb, N_META + 128 * blk, 0)
    w = 2.0 * math.pi / n
    pa = ((pk[:, None] * base[None, :]) % n).astype(F32) * w
    pb = ((pk[:, None] * jnp.arange(128, dtype=jnp.int32)[None, :]) % n).astype(F32) * w
    ca, sa, cb, sb = jnp.cos(pa), jnp.sin(pa), jnp.cos(pb), jnp.sin(pb)
    scale = (weight[:, None] * weight[None, :]) * n ** -0.5
    c = (ca[:, :, None] * cb[:, None, :] - sa[:, :, None] * sb[:, None, :]).reshape(hl, hl) * scale
    s = (sa[:, :, None] * cb[:, None, :] + ca[:, :, None] * sb[:, None, :]).reshape(hl, hl) * scale
    return c.astype(BF16), s.astype(BF16)


def _group_dft_mats():
    i = jnp.arange(GROUP_DIM, dtype=jnp.int32)
    ang = ((i[:, None] * i[None, :]) % GROUP_DIM).astype(F32) * (2.0 * math.pi / GROUP_DIM)
    eye2 = jnp.eye(2, dtype=F32)
    cg = jnp.kron(eye2, jnp.cos(ang)) * GROUP_DIM ** -0.5
    sg = jnp.kron(eye2, jnp.sin(ang)) * GROUP_DIM ** -0.5
    return cg.astype(BF16), sg.astype(BF16)


def _mirror_select():
    r = lax.broadcasted_iota(jnp.int32, (128, 256), 0)
    c = lax.broadcasted_iota(jnp.int32, (128, 256), 1)
    rm = lax.broadcasted_iota(jnp.int32, (N_META, 128), 0)
    cm = lax.broadcasted_iota(jnp.int32, (N_META, 128), 1)
    return (c == 240 - r).astype(BF16), ((cm == 128 - rm) & (rm > 0)).astype(BF16)


def _fold_kernel(a_ref, b_ref, e_ref, o_ref):
    lp = a_ref.shape[1]
    seq = lp - TAIL
    sel, sel_meta = _mirror_select()
    for src, dst, sign in ((a_ref, e_ref, 1.0), (b_ref, o_ref, -1.0)):
        for i in range(seq // 256):
            win = seq - 256 - 128 * i
            rev = _dot(sel, src[0, win:win + 256, :])
            dst[0, 128 * i:128 * (i + 1), :] = (src[0, 128 * i:128 * (i + 1), :].astype(F32) + sign * rev).astype(BF16)
        rev = _dot(sel_meta, src[0, seq - 128:seq, :])
        xh = seq // 2
        dst[0, xh:xh + N_META, :] = (src[0, lp - N_META:, :].astype(F32) + sign * rev).astype(BF16)
        dst[0, xh + N_META:, :] = jnp.zeros((dst.shape[1] - xh - N_META, dst.shape[2]), BF16)


def _unfold_kernel(d_ref, s_ref, y_ref):
    lp = y_ref.shape[1]
    seq = lp - TAIL
    fw = y_ref.shape[2]
    nxb = seq // 256
    xh = seq // 2
    sel, sel_meta = _mirror_select()
    for j in range(seq // 128):
        y = d_ref[0, 128 * j:128 * (j + 1), :].astype(F32) if j < nxb else None
        i1 = seq // 128 - 2 - j
        ok1, ok2 = 0 <= i1 < nxb, 0 <= i1 + 1 < nxb
        if ok1 and ok2:
            hi = _dot(sel, s_ref[0, 128 * i1:128 * i1 + 256, :])
        elif ok1:
            hi = _dot(sel[:, :128], s_ref[0, 128 * i1:128 * i1 + 128, :])
        elif ok2:
            hi = _dot(sel[:, 128:], s_ref[0, 128 * (i1 + 1):128 * (i1 + 2), :])
        else:
            hi = None
        if j == seq // 128 - 1:
            hi = hi + _dot_tn(sel_meta, s_ref[0, xh:xh + N_META, :])
        y = hi if y is None else (y if hi is None else y + hi)
        y_ref[0, 128 * j:128 * (j + 1), :] = y.astype(BF16)
    y_ref[0, seq:lp - N_META, :] = jnp.zeros((TAIL - N_META, fw), BF16)
    y_ref[0, lp - N_META:, :] = d_ref[0, xh:xh + N_META, :]


def _dft_kernel(c_ref, s_ref, e_ref, o_ref, d_ref, m_ref):
    c = c_ref[...]
    s = s_ref[...]
    for i in range(e_ref.shape[0]):
        p = _dot(c, e_ref[i])
        q = _dot(s, o_ref[i])
        d_ref[i] = (p - q).astype(BF16)
        m_ref[i] = (p + q).astype(BF16)


def _fourier_positions(ctab, stab, uc, us):
    bn, lp, fw = uc.shape
    hl = ctab.shape[0]
    whole = lambda n: pl.BlockSpec((1, n, fw), lambda i: (i, 0, 0))
    half = jax.ShapeDtypeStruct((bn, hl, fw), BF16)
    e, o = pl.pallas_call(
        _fold_kernel, grid=(bn,), in_specs=[whole(lp), whole(lp)], out_specs=[whole(hl), whole(hl)],
        out_shape=[half, half], compiler_params=_cparams("parallel"), name="fourier_fold",
    )(uc, us)
    nb = 2 if bn % 2 == 0 else 1
    tm = _pick_tile(hl, 768, 128)
    d, m = pl.pallas_call(
        _dft_kernel,
        grid=(bn // nb, hl // tm),
        in_specs=[pl.BlockSpec((tm, hl), lambda g, i: (i, 0)),
                  pl.BlockSpec((tm, hl), lambda g, i: (i, 0)),
                  pl.BlockSpec((nb, hl, fw), lambda g, i: (g, 0, 0)),
                  pl.BlockSpec((nb, hl, fw), lambda g, i: (g, 0, 0))],
        out_specs=[pl.BlockSpec((nb, tm, fw), lambda g, i: (g, i, 0))] * 2,
        out_shape=[half, half],
        compiler_params=_cparams("parallel", "parallel"),
        name="fourier_positions",
    )(ctab, stab, e, o)
    return pl.pallas_call(
        _unfold_kernel, grid=(bn,), in_specs=[whole(hl), whole(hl)], out_specs=whole(lp),
        out_shape=jax.ShapeDtypeStruct((bn, lp, fw), BF16), compiler_params=_cparams("parallel"),
        name="fourier_unfold",
    )(d, m)


def _conv_kernel(seq, qm_ref, qp_ref, qn_ref, gcol_ref, grow_ref, shift_ref, w_ref, ac_ref, dc_ref, ar_ref, dr_ref,
                 q_ref, k_ref, v_ref, gco_ref, gro_ref, xx_ref):
    tm = qm_ref.shape[1]
    j = pl.program_id(1)
    lp = seq + TAIL

    def pad_mask(start, n):
        p = start + lax.broadcasted_iota(jnp.int32, (n, 1), 0)
        p = jnp.where(p < 0, p + lp, jnp.where(p >= lp, p - lp, p))
        return (p < seq) | (p >= lp - N_META)

    live = pad_mask(j * tm, tm)
    zero = jnp.zeros((), BF16)
    xx_ref[:HALO, :] = jnp.where(pad_mask(j * tm - HALO, HALO), qp_ref[0], zero)
    xx_ref[HALO:HALO + tm, :] = jnp.where(live, qm_ref[0], zero)
    xx_ref[HALO + tm:2 * HALO + tm, :] = jnp.where(pad_mask(j * tm + tm, HALO), qn_ref[0], zero)
    xx_ref[2 * HALO + tm:, :] = jnp.zeros((xx_ref.shape[0] - 2 * HALO - tm, xx_ref.shape[1]), BF16)

    half = CONV_WIDTH // 2
    taps = [t for t in range(CONV_WIDTH) if t != half]
    for part, dst in enumerate((q_ref, k_ref, v_ref)):
        cols = slice(part * DW, (part + 1) * DW)
        for i in range(tm // 128):
            shifted = _dot(shift_ref[...], xx_ref[128 * i:128 * i + 256, cols])
            y = xx_ref[HALO + 128 * i:HALO + 128 * (i + 1), cols].astype(F32) * w_ref[half:half + 1, cols]
            for n, t in enumerate(taps):
                y = y + shifted[128 * n:128 * (n + 1)] * w_ref[t:t + 1, cols]
            y = jnp.where(pad_mask(j * tm + 128 * i, 128), y * _sigmoid(y), 0.0)
            rows = slice(128 * i, 128 * (i + 1))
            if part == 2:
                dst[0, rows, :] = y.astype(BF16)
                continue
            scale = HEAD_DIM ** -0.5 if part == 0 else 1.0
            for h in range(N_HEADS):
                lo = h * HEAD_DIM
                yh = y[:, lo:lo + HEAD_DIM]
                yh = yh * (lax.rsqrt(jnp.sum(yh * yh, -1, keepdims=True) + L2_EPS) * scale)
                dst[0, rows, lo:lo + HEAD_DIM] = yh.astype(BF16)

    gl = gcol_ref[0]
    beta = jnp.where(live, _sigmoid(gl[:, :NGATE]), 0.0)
    g = jnp.where(live, -jnp.exp(ac_ref[...]) * _softplus(gl[:, NGATE:] + dc_ref[...]), 0.0)
    i0 = lax.broadcasted_iota(jnp.int32, (CHUNK, CHUNK), 0)
    i1 = lax.broadcasted_iota(jnp.int32, (CHUNK, CHUNK), 1)
    tri_lo = (i0 >= i1).astype(BF16)
    tri_up = (i0 <= i1).astype(BF16)
    is_fwd_col = lax.broadcasted_iota(jnp.int32, (1, NGATE), 1) < N_HEADS
    g1 = g.astype(BF16)
    g2 = (g - g1.astype(F32)).astype(BF16)
    g3 = (g - g1.astype(F32) - g2.astype(F32)).astype(BF16)
    for c in range(tm // CHUNK):
        sl = slice(c * CHUNK, (c + 1) * CHUNK)
        pre = _dot(tri_lo, g1[sl]) + _dot(tri_lo, g2[sl]) + _dot(tri_lo, g3[sl])
        suf = _dot(tri_up, g1[sl]) + _dot(tri_up, g2[sl]) + _dot(tri_up, g3[sl])
        gco_ref[0, sl, :NGATE] = beta[sl]
        gco_ref[0, sl, NGATE:] = jnp.where(is_fwd_col, pre, suf)

    nc = grow_ref.shape[0]
    glr = grow_ref[:, NGATE:, :]
    shp = (nc, NGATE, CHUNK)
    pr = j * tm + lax.broadcasted_iota(jnp.int32, shp, 0) * CHUNK + lax.broadcasted_iota(jnp.int32, shp, 2)
    live_r = (pr < seq) | (pr >= lp - N_META)
    gr = jnp.where(live_r, -jnp.exp(ar_ref[...])[None] * _softplus(glr + dr_ref[...][None]), 0.0)
    gr = gr.reshape(nc * NGATE, CHUNK)
    r1 = gr.astype(BF16)
    r2 = (gr - r1.astype(F32)).astype(BF16)
    r3 = (gr - r1.astype(F32) - r2.astype(F32)).astype(BF16)
    pre_r = _dot(r1, tri_up) + _dot(r2, tri_up) + _dot(r3, tri_up)
    suf_r = _dot(r1, tri_lo) + _dot(r2, tri_lo) + _dot(r3, tri_lo)
    is_fwd_row = lax.broadcasted_iota(jnp.int32, shp, 1) < N_HEADS
    gro_ref[...] = jnp.where(is_fwd_row, pre_r.reshape(shp), suf_r.reshape(shp))


def _conv_and_gates(qkv, gcol, grow, conv_w, a_log, dt_bias, seq):
    bn, lp, qw = qkv.shape
    tm = _pick_tile(lp, 1408, 128)
    nt = lp // tm
    nc = tm // CHUNK
    hb = tm // HALO
    nhb = lp // HALO
    a_flat = a_log.reshape(1, NGATE).astype(F32)
    d_flat = dt_bias.reshape(1, NGATE).astype(F32)
    small = lambda shape: pl.BlockSpec(shape, lambda i, j: (0,) * len(shape))
    tok = lambda n: pl.BlockSpec((1, tm, n), lambda i, j: (i, j, 0))
    half = CONV_WIDTH // 2
    rr = jnp.arange(128, dtype=jnp.int32)[:, None]
    cc = jnp.arange(256, dtype=jnp.int32)[None, :]
    shift = jnp.concatenate([(cc == rr + HALO + t - half) for t in range(CONV_WIDTH) if t != half], 0).astype(BF16)
    return pl.pallas_call(
        functools.partial(_conv_kernel, seq),
        grid=(bn, nt),
        in_specs=[tok(qw),
                  pl.BlockSpec((1, HALO, qw), lambda i, j: (i, (j * hb + nhb - 1) % nhb, 0)),
                  pl.BlockSpec((1, HALO, qw), lambda i, j: (i, ((j + 1) * hb) % nhb, 0)),
                  tok(2 * NGATE),
                  pl.BlockSpec((nc, 2 * NGATE, CHUNK), lambda i, j: (i * nt + j, 0, 0)),
                  small(shift.shape),
                  small((CONV_WIDTH, qw)), small((1, NGATE)), small((1, NGATE)),
                  small((NGATE, 1)), small((NGATE, 1))],
        out_specs=[tok(DW), tok(DW), tok(DW), tok(2 * NGATE),
                   pl.BlockSpec((nc, NGATE, CHUNK), lambda i, j: (i * nt + j, 0, 0))],
        out_shape=[jax.ShapeDtypeStruct((bn, lp, DW), BF16)] * 3
        + [jax.ShapeDtypeStruct((bn, lp, 2 * NGATE), F32),
           jax.ShapeDtypeStruct((bn * lp // CHUNK, NGATE, CHUNK), F32)],
        scratch_shapes=[pltpu.VMEM((tm + 128, qw), BF16)],
        compiler_params=_cparams("parallel", "parallel"),
        name="conv_gates",
    )(qkv, qkv, qkv, gcol, grow, shift, conv_w.astype(F32), a_flat, d_flat, a_flat.T, d_flat.T)


class _Chain:
    pass


def _delta_kernel(qf_ref, kf_ref, vf_ref, cf_ref, rf_ref, qb_ref, kb_ref, vb_ref, cb_ref, rb_ref,
                  of_ref, ob_ref, s_ref):
    step = pl.program_id(1)
    nb = qf_ref.shape[0]

    @pl.when(step == 0)
    def _():
        s_ref[...] = jnp.zeros_like(s_ref)

    i0 = lax.broadcasted_iota(jnp.int32, (CHUNK, CHUNK), 0)
    i1 = lax.broadcasted_iota(jnp.int32, (CHUNK, CHUNK), 1)
    eye = (i0 == i1).astype(F32)
    level_masks = []
    b = 1
    while b < CHUNK:
        level_masks.append((((i0 // (2 * b)) == (i1 // (2 * b))) & ((i0 // b) != (i1 // b))).astype(F32))
        b *= 2

    dirs = ((qf_ref, kf_ref, vf_ref, cf_ref, rf_ref, of_ref), (qb_ref, kb_ref, vb_ref, cb_ref, rb_ref, ob_ref))
    chains = []
    for bi in range(nb):
        for r, (q_ref, k_ref, v_ref, c_ref, r_ref, o_ref) in enumerate(dirs):
            incl = (i0 >= i1) if r == 0 else (i0 <= i1)
            strict = (i0 > i1) if r == 0 else (i0 < i1)
            last = CHUNK - 1 if r == 0 else 0
            gcol = c_ref[bi]
            grow = r_ref[bi, 0]
            for h in range(N_HEADS):
                ch = _Chain()
                c = r * N_HEADS + h
                lo = h * HEAD_DIM
                ch.o_ref, ch.bi, ch.lo, ch.slot = o_ref, bi, lo, bi * NGATE + c
                ch.q = q_ref[bi, :, lo:lo + HEAD_DIM]
                ch.k = k_ref[bi, :, lo:lo + HEAD_DIM]
                v = v_ref[bi, :, lo:lo + HEAD_DIM].astype(F32)
                kf32 = ch.k.astype(F32)
                beta = gcol[:, c:c + 1]
                gc = gcol[:, NGATE + c:NGATE + c + 1]
                gr = grow[c:c + 1, :]
                g_last = gcol[last:last + 1, NGATE + c:NGATE + c + 1]
                e_gc = jnp.exp(gc)
                ch.decay = jnp.exp(jnp.where(incl, gc - gr, -1e30))
                ch.decay_strict = jnp.where(strict, ch.decay, 0.0)
                kbeta = kf32 * beta
                ch.kbeta = kbeta.astype(BF16)
                ch.rhs = jnp.concatenate([v * beta, kbeta * e_gc], axis=-1).astype(BF16)
                ch.q_decay = (ch.q.astype(F32) * e_gc).astype(BF16)
                ch.k_tail = (kf32 * jnp.exp(g_last - gc)).astype(BF16)
                ch.chunk_decay = jnp.exp(g_last)
                chains.append(ch)

    for ch in chains:
        both = _dot_nt(jnp.concatenate([ch.kbeta, ch.q], axis=0), ch.k)
        ch.lmat = both[:CHUNK] * ch.decay_strict
        ch.attn = (both[CHUNK:] * ch.decay).astype(BF16)
        ch.x = eye - ch.lmat * level_masks[0]
    for m in level_masks[1:]:
        for ch in chains:
            ch.xb = ch.x.astype(BF16)
            ch.t = _dot(ch.xb, (ch.lmat * m).astype(BF16)).astype(BF16)
        for ch in chains:
            ch.x = ch.x - _dot(ch.t, ch.xb)
    for ch in chains:
        ch.sol = _dot(ch.x.astype(BF16), ch.rhs)
    for ch in chains:
        ch.s_old = s_ref[ch.slot]
        ch.sb = ch.s_old.astype(BF16)
        lhs = jnp.concatenate([ch.sol[:, HEAD_DIM:].astype(BF16), ch.q_decay], axis=0)
        ch.ks = _dot(lhs, ch.sb)
    for ch in chains:
        v_new = (ch.sol[:, :HEAD_DIM] - ch.ks[:CHUNK]).astype(BF16)
        o = ch.ks[CHUNK:] + _dot(ch.attn, v_new)
        s_ref[ch.slot] = ch.s_old * ch.chunk_decay + _dot_tn(ch.k_tail, v_new)
        ch.o_ref[ch.bi, :, ch.lo:ch.lo + HEAD_DIM] = o.astype(BF16)


def _delta_rule(q, k, v, gco, gro, seq):
    bn, lp, dw = q.shape
    nch = lp // CHUNK
    nx = seq // CHUNK
    ntail = nch - nx

    def fwd(s):
        return jnp.where(s < ntail, nch - 1 - s, s - ntail)

    def bwd(s):
        return jnp.where(s < nx, nx - 1 - s, s)

    nb = DELTA_BATCHES_PER_STEP if bn % DELTA_BATCHES_PER_STEP == 0 else 1
    gro = gro.reshape(bn, nch, NGATE, CHUNK)

    def tok(order):
        return pl.BlockSpec((nb, CHUNK, dw), lambda i, s: (i, order(s), 0))

    def col(order):
        return pl.BlockSpec((nb, CHUNK, 2 * NGATE), lambda i, s: (i, order(s), 0))

    def row(order):
        return pl.BlockSpec((nb, 1, NGATE, CHUNK), lambda i, s: (i, order(s), 0, 0))

    return pl.pallas_call(
        _delta_kernel,
        grid=(bn // nb, nch),
        in_specs=[tok(fwd), tok(fwd), tok(fwd), col(fwd), row(fwd),
                  tok(bwd), tok(bwd), tok(bwd), col(bwd), row(bwd)],
        out_specs=[tok(fwd), tok(bwd)],
        out_shape=[jax.ShapeDtypeStruct((bn, lp, dw), BF16)] * 2,
        scratch_shapes=[pltpu.VMEM((nb * NGATE, HEAD_DIM, HEAD_DIM), F32)],
        compiler_params=_cparams("parallel", "arbitrary"),
        name="delta_rule",
    )(q, k, v, gco, gro, q, k, v, gco, gro)


def _post_kernel(alpha, y_ref, of_ref, ob_ref, z_ref, gate_ref, h_ref, wf_ref, wd_ref, wo_ref,
                 ng_ref, g_ref, b_ref, o_ref):
    d = h_ref.shape[-1]
    tm = h_ref.shape[0]
    sub = _pick_tile(tm, SUB_ROWS, 128)
    ng = ng_ref[...]
    for s in range(tm // sub):
        rows = slice(s * sub, (s + 1) * sub)
        heads = []
        for hh in range(N_HEADS):
            sl = slice(hh * HEAD_DIM, (hh + 1) * HEAD_DIM)
            o = of_ref[rows, sl].astype(F32) + ob_ref[rows, sl].astype(F32)
            z = z_ref[rows, sl].astype(F32)
            o = o * lax.rsqrt(jnp.mean(o * o, -1, keepdims=True) + RMS_EPS) * ng * (z * _sigmoid(z))
            heads.append(o.astype(BF16))
        y_a = _dot(y_ref[rows, :], wf_ref[...])
        y_b = _dot(jnp.concatenate(heads, axis=-1), wd_ref[...])
        m = gate_ref[rows, :d].astype(F32) * y_a + gate_ref[rows, d:].astype(F32) * y_b
        mix = _dot(m.astype(BF16), wo_ref[...])
        o_ref[rows, :] = _layer_norm(alpha * h_ref[rows, :] + mix, g_ref[...], b_ref[...])


def _merge_and_project(alpha, y, o_f, o_b, z, gates, h, wf, wd, wo, norm_g, ln_g, ln_b):
    r, d = h.shape
    tm = _pick_tile(r, ROW_TILE, 128)
    row = lambda n: pl.BlockSpec((tm, n), lambda i: (i, 0))
    return pl.pallas_call(
        functools.partial(_post_kernel, alpha),
        grid=(r // tm,),
        in_specs=[row(FW), row(DW), row(DW), row(DW), row(2 * d), row(d),
                  _resident(wf.shape), _resident(wd.shape), _resident(wo.shape),
                  _resident((1, HEAD_DIM)), _resident((1, d)), _resident((1, d))],
        out_specs=row(d),
        out_shape=jax.ShapeDtypeStruct((r, d), F32),
        compiler_params=_cparams("parallel"),
        name="merge_project",
    )(y, o_f, o_b, z, gates, h, wf, wd, wo, norm_g.reshape(1, HEAD_DIM).astype(F32),
      ln_g.reshape(1, d).astype(F32), ln_b.reshape(1, d).astype(F32))


def _ffn_kernel(alpha, fchunk, x_ref, wu_ref, wd_ref, g_ref, b_ref, o_ref, acc_ref):
    x = x_ref[0]
    xb = x.astype(BF16)
    nf = wu_ref.shape[1] // fchunk
    for f in range(nf):
        sl = slice(f * fchunk, (f + 1) * fchunk)
        a = jnp.maximum(_dot(xb, wu_ref[:, sl]), 0.0)
        part = _dot((a * a).astype(BF16), wd_ref[sl, :])
        if f == 0:
            acc_ref[...] = part
        else:
            acc_ref[...] += part
    o_ref[0] = _layer_norm(alpha * x + acc_ref[...], g_ref[...], b_ref[...])


def _ffn(alpha, x, wu, wd, ln_g, ln_b, rows_out):
    bn, lp, d = x.shape
    dff = wu.shape[1]
    tm = _pick_tile(rows_out, ROW_TILE, 128)
    tok = pl.BlockSpec((1, tm, d), lambda i, j: (i, j, 0))
    return pl.pallas_call(
        functools.partial(_ffn_kernel, alpha, min(dff, 512)),
        grid=(bn, rows_out // tm),
        in_specs=[tok, _resident(wu.shape), _resident(wd.shape), _resident((1, d)), _resident((1, d))],
        out_specs=tok,
        out_shape=jax.ShapeDtypeStruct((bn, rows_out, d), F32),
        scratch_shapes=[pltpu.VMEM((tm, d), F32)],
        compiler_params=_cparams("parallel", "parallel"),
        name="ffn",
    )(x, wu, wd, ln_g.reshape(1, d).astype(F32), ln_b.reshape(1, d).astype(F32))


def kernel(x, meta_tokens, ln0_g, ln0_b, w_in, conv_w, a_log, dt_bias, delta_norm_g, w_fourier_proj,
           w_delta_proj, w_out, ln1_g, ln1_b, w_up, w_down, ln2_g, ln2_b):
    bn, seq, d = x.shape
    depth = w_in.shape[0]
    assert seq % 256 == 0 and d == 8 * GROUP_DIM
    lp = seq + TAIL
    r = bn * lp
    alpha = (2 * depth) ** 0.25

    h = _input_norm(x, meta_tokens.astype(x.dtype), ln0_g, ln0_b)
    ctab, stab = _dft_tables(seq)
    cg2, sg2 = _group_dft_mats()
    w_in, w_fourier_proj, w_delta_proj, w_out, w_up, w_down = [
        w.astype(BF16) for w in (w_in, w_fourier_proj, w_delta_proj, w_out, w_up, w_down)]
    for l in range(depth):
        uc, us, qkv, gcol, grow, z, gates = _in_projection(h.reshape(r, d), w_in[l], cg2, sg2)
        y = _fourier_positions(ctab, stab, uc.reshape(bn, lp, FW), us.reshape(bn, lp, FW))
        q, k, v, gco, gro = _conv_and_gates(qkv.reshape(bn, lp, 3 * DW), gcol.reshape(bn, lp, 2 * NGATE),
                                            grow, conv_w[l], a_log[l], dt_bias[l], seq)
        o_f, o_b = _delta_rule(q, k, v, gco, gro, seq)
        x1 = _merge_and_project(alpha, y.reshape(r, FW), o_f.reshape(r, DW), o_b.reshape(r, DW), z, gates,
                                h.reshape(r, d), w_fourier_proj[l], w_delta_proj[l], w_out[l],
                                delta_norm_g[l], ln1_g[l], ln1_b[l])
        last = l == depth - 1
        if last:
            return _ffn(alpha, x1.reshape(bn, lp, d), w_up[l], w_down[l], ln2_g[l], ln2_b[l], seq)
        h = _ffn(alpha, x1.reshape(1, r, d), w_up[l], w_down[l], ln2_g[l], ln2_b[l], r)
```

```python
import functools
import math

import jax
import jax.numpy as jnp
from jax import lax
from jax.experimental import pallas as pl
from jax.experimental.pallas import tpu as pltpu

N_META = 16
N_GROUPS = 4
GROUP_DIM = 128
N_HEADS = 4
HEAD_DIM = 128
N_DIRS = 2
CONV_WIDTH = 5
CHUNK = 64
TAIL = 128
HALO = 16
LN_EPS = 1e-5
RMS_EPS = 1e-6
L2_EPS = 1e-6
FW = N_GROUPS * GROUP_DIM
DW = N_HEADS * HEAD_DIM
NGATE = N_DIRS * N_HEADS
VMEM_LIMIT = 56 * 1024 * 1024
DELTA_BATCHES_PER_STEP = 4
ROW_TILE = 1024
SUB_ROWS = 256

F32 = jnp.float32
BF16 = jnp.bfloat16


def _cparams(*sem):
    return pltpu.CompilerParams(dimension_semantics=sem, vmem_limit_bytes=VMEM_LIMIT)


def _dot(a, b):
    return jnp.dot(a, b, preferred_element_type=F32)


def _dot_nt(a, b):
    return lax.dot_general(a, b, (((1,), (1,)), ((), ())), preferred_element_type=F32)


def _dot_tn(a, b):
    return lax.dot_general(a, b, (((0,), (0,)), ((), ())), preferred_element_type=F32)


def _layer_norm(x, g, b):
    mu = jnp.mean(x, -1, keepdims=True)
    xc = x - mu
    var = jnp.mean(xc * xc, -1, keepdims=True)
    return xc * lax.rsqrt(var + LN_EPS) * g + b


def _sigmoid(x):
    return 1.0 / (1.0 + jnp.exp(-x))


def _softplus(x):
    return jnp.maximum(x, 0.0) + jnp.log1p(jnp.exp(-jnp.abs(x)))


def _pick_tile(n, target, mult):
    best = None
    for t in range(mult, min(n, target) + 1, mult):
        if n % t == 0:
            best = t
    assert best is not None, (n, target, mult)
    return best


def _resident(shape):
    nd = len(shape)
    return pl.BlockSpec(shape, lambda *_: (0,) * nd, pipeline_mode=pl.Buffered(1))


def _resident_layer(stack, layer):
    return pl.BlockSpec((None,) + stack.shape[1:], lambda *_: (layer, 0, 0), pipeline_mode=pl.Buffered(1))


def _ln0_kernel(x_ref, meta_ref, g_ref, b_ref, o_ref):
    j = pl.program_id(1)
    t0, d = x_ref.shape[1], x_ref.shape[2]
    seq = o_ref.shape[1] - TAIL
    o_ref[0, pl.ds(pl.multiple_of(j * t0, t0), t0), :] = _layer_norm(x_ref[0], g_ref[...], b_ref[...])

    @pl.when(j == 0)
    def _():
        o_ref[0, seq:seq + TAIL - N_META, :] = jnp.zeros((TAIL - N_META, d), F32)
        o_ref[0, seq + TAIL - N_META:, :] = _layer_norm(meta_ref[...], g_ref[...], b_ref[...])


def _input_norm(x, meta, g, b):
    bn, seq, d = x.shape
    lp = seq + TAIL
    t0 = _pick_tile(seq, 1024, 128)
    return pl.pallas_call(
        _ln0_kernel,
        grid=(bn, seq // t0),
        in_specs=[pl.BlockSpec((1, t0, d), lambda i, j: (i, j, 0)),
                  pl.BlockSpec((N_META, d), lambda i, j: (0, 0)),
                  pl.BlockSpec((1, d), lambda i, j: (0, 0)),
                  pl.BlockSpec((1, d), lambda i, j: (0, 0))],
        out_specs=pl.BlockSpec((1, lp, d), lambda i, j: (i, 0, 0)),
        out_shape=jax.ShapeDtypeStruct((bn, lp, d), F32),
        compiler_params=_cparams("parallel", "arbitrary"),
        name="ln0",
    )(x, meta, g.reshape(1, d), b.reshape(1, d))


def _inproj_kernel(h_ref, wu_ref, wqkv_ref, wgc_ref, wgr_ref, wz_ref, wg_ref, cg_ref, sg_ref,
                   uc_ref, us_ref, qkv_ref, gcol_ref, grow_ref, z_ref, gate_ref):
    x = h_ref[...].astype(BF16)
    u = _dot(x, wu_ref[...]).astype(BF16)
    half = 2 * GROUP_DIM
    for i in range(FW // half):
        ui = u[:, i * half:(i + 1) * half]
        uc_ref[:, i * half:(i + 1) * half] = _dot(ui, cg_ref[...]).astype(BF16)
        us_ref[:, i * half:(i + 1) * half] = _dot(ui, sg_ref[...]).astype(BF16)
    qkv_ref[...] = _dot(x, wqkv_ref[...]).astype(BF16)
    gcol_ref[...] = _dot(x, wgc_ref[...])
    gr = _dot_nt(wgr_ref[...], x)
    for c in range(grow_ref.shape[0]):
        grow_ref[c] = gr[:, c * CHUNK:(c + 1) * CHUNK]
    z_ref[...] = _dot(x, wz_ref[...]).astype(BF16)
    gate_ref[...] = _sigmoid(_dot(x, wg_ref[...])).astype(BF16)


def _in_projection(h, wb, cg2, sg2):
    r, d = h.shape
    tm = _pick_tile(r, ROW_TILE, 128)
    qw = 3 * DW
    o0, o1, o2, o3 = FW, FW + qw, FW + qw + 2 * NGATE, FW + qw + 2 * NGATE + DW
    wu, wqkv, wgc, wz, wg = wb[:, :o0], wb[:, o0:o1], wb[:, o1:o2], wb[:, o2:o3], wb[:, o3:]
    wgr = wgc.T
    gw = wg.shape[1]
    row = lambda n: pl.BlockSpec((tm, n), lambda i: (i, 0))
    outs = pl.pallas_call(
        _inproj_kernel,
        grid=(r // tm,),
        in_specs=[row(d), _resident(wu.shape), _resident(wqkv.shape), _resident(wgc.shape),
                  _resident(wgr.shape), _resident(wz.shape), _resident(wg.shape),
                  _resident(cg2.shape), _resident(sg2.shape)],
        out_specs=[row(FW), row(FW), row(qw), row(2 * NGATE),
                   pl.BlockSpec((tm // CHUNK, 2 * NGATE, CHUNK), lambda i: (i, 0, 0)),
                   row(DW), row(gw)],
        out_shape=[jax.ShapeDtypeStruct((r, FW), BF16), jax.ShapeDtypeStruct((r, FW), BF16),
                   jax.ShapeDtypeStruct((r, qw), BF16), jax.ShapeDtypeStruct((r, 2 * NGATE), F32),
                   jax.ShapeDtypeStruct((r // CHUNK, 2 * NGATE, CHUNK), F32),
                   jax.ShapeDtypeStruct((r, DW), BF16), jax.ShapeDtypeStruct((r, gw), BF16)],
        compiler_params=_cparams("parallel"),
        name="in_projection",
    )(h, wu, wqkv, wgc, wgr, wz, wg, cg2, sg2)
    return outs


def _half_rows(seq):
    return seq // 2 + 256


def _dft_tables(seq):
    n = seq + N_META
    hl = _half_rows(seq)
    nblk = hl // 128
    nxb = seq // 256
    m = jnp.arange(hl, dtype=jnp.int32)
    self_row = seq // 2 - N_META // 2
    pos = jnp.where(m < seq // 2, m + N_META, m - seq // 2)
    valid = (m <= self_row) | ((m >= seq // 2) & (m < seq // 2 + N_META))
    weight = jnp.where(valid, jnp.where(m == self_row, 0.5, 1.0), 0.0).astype(F32)
    pk = jnp.where(valid, pos, 0)
    blk = jnp.arange(nblk, dtype=jnp.int32)
    base = jnp.where(blk < nxb, N_META + 128 * blk, 0)
    w = 2.0 * math.pi / n
    pa = ((pk[:, None] * base[None, :]) % n).astype(F32) * w
    pb = ((pk[:, None] * jnp.arange(128, dtype=jnp.int32)[None, :]) % n).astype(F32) * w
    ca, sa, cb, sb = jnp.cos(pa), jnp.sin(pa), jnp.cos(pb), jnp.sin(pb)
    scale = (weight[:, None] * weight[None, :]) * n ** -0.5
    c = (ca[:, :, None] * cb[:, None, :] - sa[:, :, None] * sb[:, None, :]).reshape(hl, hl) * scale
    s = (sa[:, :, None] * cb[:, None, :] + ca[:, :, None] * sb[:, None, :]).reshape(hl, hl) * scale
    return c.astype(BF16), s.astype(BF16)


def _group_dft_mats():
    i = jnp.arange(GROUP_DIM, dtype=jnp.int32)
    ang = ((i[:, None] * i[None, :]) % GROUP_DIM).astype(F32) * (2.0 * math.pi / GROUP_DIM)
    eye2 = jnp.eye(2, dtype=F32)
    cg = jnp.kron(eye2, jnp.cos(ang)) * GROUP_DIM ** -0.5
    sg = jnp.kron(eye2, jnp.sin(ang)) * GROUP_DIM ** -0.5
    return cg.astype(BF16), sg.astype(BF16)


def _mirror_select():
    r = lax.broadcasted_iota(jnp.int32, (128, 256), 0)
    c = lax.broadcasted_iota(jnp.int32, (128, 256), 1)
    rm = lax.broadcasted_iota(jnp.int32, (N_META, 128), 0)
    cm = lax.broadcasted_iota(jnp.int32, (N_META, 128), 1)
    return (c == 240 - r).astype(BF16), ((cm == 128 - rm) & (rm > 0)).astype(BF16)


def _fold_kernel(a_ref, b_ref, e_ref, o_ref):
    lp = a_ref.shape[1]
    seq = lp - TAIL
    sel, sel_meta = _mirror_select()
    for src, dst, sign in ((a_ref, e_ref, 1.0), (b_ref, o_ref, -1.0)):
        for i in range(seq // 256):
            win = seq - 256 - 128 * i
            rev = _dot(sel, src[0, win:win + 256, :])
            dst[0, 128 * i:128 * (i + 1), :] = (src[0, 128 * i:128 * (i + 1), :].astype(F32) + sign * rev).astype(BF16)
        rev = _dot(sel_meta, src[0, seq - 128:seq, :])
        xh = seq // 2
        dst[0, xh:xh + N_META, :] = (src[0, lp - N_META:, :].astype(F32) + sign * rev).astype(BF16)
        dst[0, xh + N_META:, :] = jnp.zeros((dst.shape[1] - xh - N_META, dst.shape[2]), BF16)


def _unfold_kernel(d_ref, s_ref, y_ref):
    lp = y_ref.shape[1]
    seq = lp - TAIL
    fw = y_ref.shape[2]
    nxb = seq // 256
    xh = seq // 2
    sel, sel_meta = _mirror_select()
    for j in range(seq // 128):
        y = d_ref[0, 128 * j:128 * (j + 1), :].astype(F32) if j < nxb else None
        i1 = seq // 128 - 2 - j
        ok1, ok2 = 0 <= i1 < nxb, 0 <= i1 + 1 < nxb
        if ok1 and ok2:
            hi = _dot(sel, s_ref[0, 128 * i1:128 * i1 + 256, :])
        elif ok1:
            hi = _dot(sel[:, :128], s_ref[0, 128 * i1:128 * i1 + 128, :])
        elif ok2:
            hi = _dot(sel[:, 128:], s_ref[0, 128 * (i1 + 1):128 * (i1 + 2), :])
        else:
            hi = None
        if j == seq // 128 - 1:
            hi = hi + _dot_tn(sel_meta, s_ref[0, xh:xh + N_META, :])
        y = hi if y is None else (y if hi is None else y + hi)
        y_ref[0, 128 * j:128 * (j + 1), :] = y.astype(BF16)
    y_ref[0, seq:lp - N_META, :] = jnp.zeros((TAIL - N_META, fw), BF16)
    y_ref[0, lp - N_META:, :] = d_ref[0, xh:xh + N_META, :]


def _dft_kernel(c_ref, s_ref, e_ref, o_ref, d_ref, m_ref):
    c = c_ref[...]
    s = s_ref[...]
    for i in range(e_ref.shape[0]):
        p = _dot(c, e_ref[i])
        q = _dot(s, o_ref[i])
        d_ref[i] = (p - q).astype(BF16)
        m_ref[i] = (p + q).astype(BF16)


def _fourier_positions(ctab, stab, uc, us):
    bn, lp, fw = uc.shape
    hl = ctab.shape[0]
    whole = lambda n: pl.BlockSpec((1, n, fw), lambda i: (i, 0, 0))
    half = jax.ShapeDtypeStruct((bn, hl, fw), BF16)
    e, o = pl.pallas_call(
        _fold_kernel, grid=(bn,), in_specs=[whole(lp), whole(lp)], out_specs=[whole(hl), whole(hl)],
        out_shape=[half, half], compiler_params=_cparams("parallel"), name="fourier_fold",
    )(uc, us)
    nb = 2 if bn % 2 == 0 else 1
    tm = _pick_tile(hl, 768, 128)
    d, m = pl.pallas_call(
        _dft_kernel,
        grid=(bn // nb, hl // tm),
        in_specs=[pl.BlockSpec((tm, hl), lambda g, i: (i, 0)),
                  pl.BlockSpec((tm, hl), lambda g, i: (i, 0)),
                  pl.BlockSpec((nb, hl, fw), lambda g, i: (g, 0, 0)),
                  pl.BlockSpec((nb, hl, fw), lambda g, i: (g, 0, 0))],
        out_specs=[pl.BlockSpec((nb, tm, fw), lambda g, i: (g, i, 0))] * 2,
        out_shape=[half, half],
        compiler_params=_cparams("parallel", "parallel"),
        name="fourier_positions",
    )(ctab, stab, e, o)
    return pl.pallas_call(
        _unfold_kernel, grid=(bn,), in_specs=[whole(hl), whole(hl)], out_specs=whole(lp),
        out_shape=jax.ShapeDtypeStruct((bn, lp, fw), BF16), compiler_params=_cparams("parallel"),
        name="fourier_unfold",
    )(d, m)


def _conv_kernel(seq, qm_ref, qp_ref, qn_ref, gcol_ref, grow_ref, shift_ref, w_ref, ac_ref, dc_ref, ar_ref, dr_ref,
                 q_ref, k_ref, v_ref, gco_ref, gro_ref, xx_ref):
    tm = qm_ref.shape[1]
    j = pl.program_id(1)
    lp = seq + TAIL

    def pad_mask(start, n):
        p = start + lax.broadcasted_iota(jnp.int32, (n, 1), 0)
        p = jnp.where(p < 0, p + lp, jnp.where(p >= lp, p - lp, p))
        return (p < seq) | (p >= lp - N_META)

    live = pad_mask(j * tm, tm)
    zero = jnp.zeros((), BF16)
    xx_ref[:HALO, :] = jnp.where(pad_mask(j * tm - HALO, HALO), qp_ref[0], zero)
    xx_ref[HALO:HALO + tm, :] = jnp.where(live, qm_ref[0], zero)
    xx_ref[HALO + tm:2 * HALO + tm, :] = jnp.where(pad_mask(j * tm + tm, HALO), qn_ref[0], zero)
    xx_ref[2 * HALO + tm:, :] = jnp.zeros((xx_ref.shape[0] - 2 * HALO - tm, xx_ref.shape[1]), BF16)

    half = CONV_WIDTH // 2
    taps = [t for t in range(CONV_WIDTH) if t != half]
    for part, dst in enumerate((q_ref, k_ref, v_ref)):
        cols = slice(part * DW, (part + 1) * DW)
        for i in range(tm // 128):
            shifted = _dot(shift_ref[...], xx_ref[128 * i:128 * i + 256, cols])
            y = xx_ref[HALO + 128 * i:HALO + 128 * (i + 1), cols].astype(F32) * w_ref[half:half + 1, cols]
            for n, t in enumerate(taps):
                y = y + shifted[128 * n:128 * (n + 1)] * w_ref[t:t + 1, cols]
            y = jnp.where(pad_mask(j * tm + 128 * i, 128), y * _sigmoid(y), 0.0)
            rows = slice(128 * i, 128 * (i + 1))
            if part == 2:
                dst[0, rows, :] = y.astype(BF16)
                continue
            scale = HEAD_DIM ** -0.5 if part == 0 else 1.0
            for h in range(N_HEADS):
                lo = h * HEAD_DIM
                yh = y[:, lo:lo + HEAD_DIM]
                yh = yh * (lax.rsqrt(jnp.sum(yh * yh, -1, keepdims=True) + L2_EPS) * scale)
                dst[0, rows, lo:lo + HEAD_DIM] = yh.astype(BF16)

    gl = gcol_ref[0]
    beta = jnp.where(live, _sigmoid(gl[:, :NGATE]), 0.0)
    g = jnp.where(live, -jnp.exp(ac_ref[...]) * _softplus(gl[:, NGATE:] + dc_ref[...]), 0.0)
    i0 = lax.broadcasted_iota(jnp.int32, (CHUNK, CHUNK), 0)
    i1 = lax.broadcasted_iota(jnp.int32, (CHUNK, CHUNK), 1)
    tri_lo = (i0 >= i1).astype(BF16)
    tri_up = (i0 <= i1).astype(BF16)
    is_fwd_col = lax.broadcasted_iota(jnp.int32, (1, NGATE), 1) < N_HEADS
    g1 = g.astype(BF16)
    g2 = (g - g1.astype(F32)).astype(BF16)
    g3 = (g - g1.astype(F32) - g2.astype(F32)).astype(BF16)
    for c in range(tm // CHUNK):
        sl = slice(c * CHUNK, (c + 1) * CHUNK)
        pre = _dot(tri_lo, g1[sl]) + _dot(tri_lo, g2[sl]) + _dot(tri_lo, g3[sl])
        suf = _dot(tri_up, g1[sl]) + _dot(tri_up, g2[sl]) + _dot(tri_up, g3[sl])
        gco_ref[0, sl, :NGATE] = beta[sl]
        gco_ref[0, sl, NGATE:] = jnp.where(is_fwd_col, pre, suf)

    nc = grow_ref.shape[0]
    glr = grow_ref[:, NGATE:, :]
    shp = (nc, NGATE, CHUNK)
    pr = j * tm + lax.broadcasted_iota(jnp.int32, shp, 0) * CHUNK + lax.broadcasted_iota(jnp.int32, shp, 2)
    live_r = (pr < seq) | (pr >= lp - N_META)
    gr = jnp.where(live_r, -jnp.exp(ar_ref[...])[None] * _softplus(glr + dr_ref[...][None]), 0.0)
    gr = gr.reshape(nc * NGATE, CHUNK)
    r1 = gr.astype(BF16)
    r2 = (gr - r1.astype(F32)).astype(BF16)
    r3 = (gr - r1.astype(F32) - r2.astype(F32)).astype(BF16)
    pre_r = _dot(r1, tri_up) + _dot(r2, tri_up) + _dot(r3, tri_up)
    suf_r = _dot(r1, tri_lo) + _dot(r2, tri_lo) + _dot(r3, tri_lo)
    is_fwd_row = lax.broadcasted_iota(jnp.int32, shp, 1) < N_HEADS
    gro_ref[...] = jnp.where(is_fwd_row, pre_r.reshape(shp), suf_r.reshape(shp))


def _conv_and_gates(qkv, gcol, grow, conv_w, a_log, dt_bias, seq):
    bn, lp, qw = qkv.shape
    tm = _pick_tile(lp, 1408, 128)
    nt = lp // tm
    nc = tm // CHUNK
    hb = tm // HALO
    nhb = lp // HALO
    a_flat = a_log.reshape(1, NGATE).astype(F32)
    d_flat = dt_bias.reshape(1, NGATE).astype(F32)
    small = lambda shape: pl.BlockSpec(shape, lambda i, j: (0,) * len(shape))
    tok = lambda n: pl.BlockSpec((1, tm, n), lambda i, j: (i, j, 0))
    half = CONV_WIDTH // 2
    rr = jnp.arange(128, dtype=jnp.int32)[:, None]
    cc = jnp.arange(256, dtype=jnp.int32)[None, :]
    shift = jnp.concatenate([(cc == rr + HALO + t - half) for t in range(CONV_WIDTH) if t != half], 0).astype(BF16)
    return pl.pallas_call(
        functools.partial(_conv_kernel, seq),
        grid=(bn, nt),
        in_specs=[tok(qw),
                  pl.BlockSpec((1, HALO, qw), lambda i, j: (i, (j * hb + nhb - 1) % nhb, 0)),
                  pl.BlockSpec((1, HALO, qw), lambda i, j: (i, ((j + 1) * hb) % nhb, 0)),
                  tok(2 * NGATE),
                  pl.BlockSpec((nc, 2 * NGATE, CHUNK), lambda i, j: (i * nt + j, 0, 0)),
                  small(shift.shape),
                  small((CONV_WIDTH, qw)), small((1, NGATE)), small((1, NGATE)),
                  small((NGATE, 1)), small((NGATE, 1))],
        out_specs=[tok(DW), tok(DW), tok(DW), tok(2 * NGATE),
                   pl.BlockSpec((nc, NGATE, CHUNK), lambda i, j: (i * nt + j, 0, 0))],
        out_shape=[jax.ShapeDtypeStruct((bn, lp, DW), BF16)] * 3
        + [jax.ShapeDtypeStruct((bn, lp, 2 * NGATE), F32),
           jax.ShapeDtypeStruct((bn * lp // CHUNK, NGATE, CHUNK), F32)],
        scratch_shapes=[pltpu.VMEM((tm + 128, qw), BF16)],
        compiler_params=_cparams("parallel", "parallel"),
        name="conv_gates",
    )(qkv, qkv, qkv, gcol, grow, shift, conv_w.astype(F32), a_flat, d_flat, a_flat.T, d_flat.T)


class _Chain:
    pass


def _delta_kernel(qf_ref, kf_ref, vf_ref, cf_ref, rf_ref, qb_ref, kb_ref, vb_ref, cb_ref, rb_ref,
                  of_ref, ob_ref, s_ref):
    step = pl.program_id(1)
    nb = qf_ref.shape[0]

    @pl.when(step == 0)
    def _():
        s_ref[...] = jnp.zeros_like(s_ref)

    i0 = lax.broadcasted_iota(jnp.int32, (CHUNK, CHUNK), 0)
    i1 = lax.broadcasted_iota(jnp.int32, (CHUNK, CHUNK), 1)
    eye = (i0 == i1).astype(F32)
    level_masks = []
    b = 1
    while b < CHUNK:
        level_masks.append((((i0 // (2 * b)) == (i1 // (2 * b))) & ((i0 // b) != (i1 // b))).astype(F32))
        b *= 2

    dirs = ((qf_ref, kf_ref, vf_ref, cf_ref, rf_ref, of_ref), (qb_ref, kb_ref, vb_ref, cb_ref, rb_ref, ob_ref))
    chains = []
    for bi in range(nb):
        for r, (q_ref, k_ref, v_ref, c_ref, r_ref, o_ref) in enumerate(dirs):
            incl = (i0 >= i1) if r == 0 else (i0 <= i1)
            strict = (i0 > i1) if r == 0 else (i0 < i1)
            last = CHUNK - 1 if r == 0 else 0
            gcol = c_ref[bi]
            grow = r_ref[bi, 0]
            for h in range(N_HEADS):
                ch = _Chain()
                c = r * N_HEADS + h
                lo = h * HEAD_DIM
                ch.o_ref, ch.bi, ch.lo, ch.slot = o_ref, bi, lo, bi * NGATE + c
                ch.q = q_ref[bi, :, lo:lo + HEAD_DIM]
                ch.k = k_ref[bi, :, lo:lo + HEAD_DIM]
                v = v_ref[bi, :, lo:lo + HEAD_DIM].astype(F32)
                kf32 = ch.k.astype(F32)
                beta = gcol[:, c:c + 1]
                gc = gcol[:, NGATE + c:NGATE + c + 1]
                gr = grow[c:c + 1, :]
                g_last = gcol[last:last + 1, NGATE + c:NGATE + c + 1]
                e_gc = jnp.exp(gc)
                ch.decay = jnp.exp(jnp.where(incl, gc - gr, -1e30))
                ch.decay_strict = jnp.where(strict, ch.decay, 0.0)
                kbeta = kf32 * beta
                ch.kbeta = kbeta.astype(BF16)
                ch.rhs = jnp.concatenate([v * beta, kbeta * e_gc], axis=-1).astype(BF16)
                ch.q_decay = (ch.q.astype(F32) * e_gc).astype(BF16)
                ch.k_tail = (kf32 * jnp.exp(g_last - gc)).astype(BF16)
                ch.chunk_decay = jnp.exp(g_last)
                chains.append(ch)

    for ch in chains:
        both = _dot_nt(jnp.concatenate([ch.kbeta, ch.q], axis=0), ch.k)
        ch.lmat = both[:CHUNK] * ch.decay_strict
        ch.attn = (both[CHUNK:] * ch.decay).astype(BF16)
        ch.x = eye - ch.lmat * level_masks[0]
    for m in level_masks[1:]:
        for ch in chains:
            ch.xb = ch.x.astype(BF16)
            ch.t = _dot(ch.xb, (ch.lmat * m).astype(BF16)).astype(BF16)
        for ch in chains:
            ch.x = ch.x - _dot(ch.t, ch.xb)
    for ch in chains:
        ch.sol = _dot(ch.x.astype(BF16), ch.rhs)
    for ch in chains:
        ch.s_old = s_ref[ch.slot]
        ch.sb = ch.s_old.astype(BF16)
        lhs = jnp.concatenate([ch.sol[:, HEAD_DIM:].astype(BF16), ch.q_decay], axis=0)
        ch.ks = _dot(lhs, ch.sb)
    for ch in chains:
        v_new = (ch.sol[:, :HEAD_DIM] - ch.ks[:CHUNK]).astype(BF16)
        o = ch.ks[CHUNK:] + _dot(ch.attn, v_new)
        s_ref[ch.slot] = ch.s_old * ch.chunk_decay + _dot_tn(ch.k_tail, v_new)
        ch.o_ref[ch.bi, :, ch.lo:ch.lo + HEAD_DIM] = o.astype(BF16)


def _delta_rule(q, k, v, gco, gro, seq):
    bn, lp, dw = q.shape
    nch = lp // CHUNK
    nx = seq // CHUNK
    ntail = nch - nx

    def fwd(s):
        return jnp.where(s < ntail, nch - 1 - s, s - ntail)

    def bwd(s):
        return jnp.where(s < nx, nx - 1 - s, s)

    nb = DELTA_BATCHES_PER_STEP if bn % DELTA_BATCHES_PER_STEP == 0 else 1
    gro = gro.reshape(bn, nch, NGATE, CHUNK)

    def tok(order):
        return pl.BlockSpec((nb, CHUNK, dw), lambda i, s: (i, order(s), 0))

    def col(order):
        return pl.BlockSpec((nb, CHUNK, 2 * NGATE), lambda i, s: (i, order(s), 0))

    def row(order):
        return pl.BlockSpec((nb, 1, NGATE, CHUNK), lambda i, s: (i, order(s), 0, 0))

    return pl.pallas_call(
        _delta_kernel,
        grid=(bn // nb, nch),
        in_specs=[tok(fwd), tok(fwd), tok(fwd), col(fwd), row(fwd),
                  tok(bwd), tok(bwd), tok(bwd), col(bwd), row(bwd)],
        out_specs=[tok(fwd), tok(bwd)],
        out_shape=[jax.ShapeDtypeStruct((bn, lp, dw), BF16)] * 2,
        scratch_shapes=[pltpu.VMEM((nb * NGATE, HEAD_DIM, HEAD_DIM), F32)],
        compiler_params=_cparams("parallel", "arbitrary"),
        name="delta_rule",
    )(q, k, v, gco, gro, q, k, v, gco, gro)


def _post_kernel(alpha, y_ref, of_ref, ob_ref, z_ref, gate_ref, h_ref, wf_ref, wd_ref, wo_ref,
                 ng_ref, g_ref, b_ref, o_ref):
    d = h_ref.shape[-1]
    tm = h_ref.shape[0]
    sub = _pick_tile(tm, SUB_ROWS, 128)
    ng = ng_ref[...]
    for s in range(tm // sub):
        rows = slice(s * sub, (s + 1) * sub)
        heads = []
        for hh in range(N_HEADS):
            sl = slice(hh * HEAD_DIM, (hh + 1) * HEAD_DIM)
            o = of_ref[rows, sl].astype(F32) + ob_ref[rows, sl].astype(F32)
            z = z_ref[rows, sl].astype(F32)
            o = o * lax.rsqrt(jnp.mean(o * o, -1, keepdims=True) + RMS_EPS) * ng * (z * _sigmoid(z))
            heads.append(o.astype(BF16))
        y_a = _dot(y_ref[rows, :], wf_ref[...])
        y_b = _dot(jnp.concatenate(heads, axis=-1), wd_ref[...])
        m = gate_ref[rows, :d].astype(F32) * y_a + gate_ref[rows, d:].astype(F32) * y_b
        mix = _dot(m.astype(BF16), wo_ref[...])
        o_ref[rows, :] = _layer_norm(alpha * h_ref[rows, :] + mix, g_ref[...], b_ref[...])


def _merge_and_project(alpha, layer, y, o_f, o_b, z, gates, h, wf, wd, wo, norm_g, ln_g, ln_b):
    r, d = h.shape
    tm = _pick_tile(r, ROW_TILE, 128)
    row = lambda n: pl.BlockSpec((tm, n), lambda i: (i, 0))
    return pl.pallas_call(
        functools.partial(_post_kernel, alpha),
        grid=(r // tm,),
        in_specs=[row(FW), row(DW), row(DW), row(DW), row(2 * d), row(d),
                  _resident_layer(wf, layer), _resident_layer(wd, layer), _resident_layer(wo, layer),
                  _resident((1, HEAD_DIM)), _resident((1, d)), _resident((1, d))],
        out_specs=row(d),
        out_shape=jax.ShapeDtypeStruct((r, d), F32),
        compiler_params=_cparams("parallel"),
        name="merge_project",
    )(y, o_f, o_b, z, gates, h, wf, wd, wo, norm_g.reshape(1, HEAD_DIM).astype(F32),
      ln_g.reshape(1, d).astype(F32), ln_b.reshape(1, d).astype(F32))


def _ffn_kernel(alpha, fchunk, x_ref, wu_ref, wd_ref, g_ref, b_ref, o_ref, acc_ref):
    x = x_ref[0]
    xb = x.astype(BF16)
    nf = wu_ref.shape[1] // fchunk
    for f in range(nf):
        sl = slice(f * fchunk, (f + 1) * fchunk)
        a = jnp.maximum(_dot(xb, wu_ref[:, sl]), 0.0)
        part = _dot((a * a).astype(BF16), wd_ref[sl, :])
        if f == 0:
            acc_ref[...] = part
        else:
            acc_ref[...] += part
    o_ref[0] = _layer_norm(alpha * x + acc_ref[...], g_ref[...], b_ref[...])


def _ffn(alpha, layer, x, wu, wd, ln_g, ln_b, rows_out):
    bn, lp, d = x.shape
    dff = wu.shape[2]
    tm = _pick_tile(rows_out, ROW_TILE, 128)
    tok = pl.BlockSpec((1, tm, d), lambda i, j: (i, j, 0))
    return pl.pallas_call(
        functools.partial(_ffn_kernel, alpha, min(dff, 512)),
        grid=(bn, rows_out // tm),
        in_specs=[tok, _resident_layer(wu, layer), _resident_layer(wd, layer), _resident((1, d)), _resident((1, d))],
        out_specs=tok,
        out_shape=jax.ShapeDtypeStruct((bn, rows_out, d), F32),
        scratch_shapes=[pltpu.VMEM((tm, d), F32)],
        compiler_params=_cparams("parallel", "parallel"),
        name="ffn",
    )(x, wu, wd, ln_g.reshape(1, d).astype(F32), ln_b.reshape(1, d).astype(F32))


def kernel(x, meta_tokens, ln0_g, ln0_b, w_in, conv_w, a_log, dt_bias, delta_norm_g, w_fourier_proj,
           w_delta_proj, w_out, ln1_g, ln1_b, w_up, w_down, ln2_g, ln2_b):
    bn, seq, d = x.shape
    depth = w_in.shape[0]
    assert seq % 256 == 0 and d == 8 * GROUP_DIM
    lp = seq + TAIL
    r = bn * lp
    alpha = (2 * depth) ** 0.25

    h = _input_norm(x, meta_tokens.astype(x.dtype), ln0_g, ln0_b)
    ctab, stab = _dft_tables(seq)
    cg2, sg2 = _group_dft_mats()
    w_in, w_fourier_proj, w_delta_proj, w_out, w_up, w_down = [
        w.astype(BF16) for w in (w_in, w_fourier_proj, w_delta_proj, w_out, w_up, w_down)]
    for l in range(depth):
        uc, us, qkv, gcol, grow, z, gates = _in_projection(h.reshape(r, d), w_in[l], cg2, sg2)
        y = _fourier_positions(ctab, stab, uc.reshape(bn, lp, FW), us.reshape(bn, lp, FW))
        q, k, v, gco, gro = _conv_and_gates(qkv.reshape(bn, lp, 3 * DW), gcol.reshape(bn, lp, 2 * NGATE),
                                            grow, conv_w[l], a_log[l], dt_bias[l], seq)
        o_f, o_b = _delta_rule(q, k, v, gco, gro, seq)
        x1 = _merge_and_project(alpha, l, y.reshape(r, FW), o_f.reshape(r, DW), o_b.reshape(r, DW), z, gates,
                                h.reshape(r, d), w_fourier_proj, w_delta_proj, w_out,
                                delta_norm_g[l], ln1_g[l], ln1_b[l])
        last = l == depth - 1
        if last:
            return _ffn(alpha, l, x1.reshape(bn, lp, d), w_up, w_down, ln2_g[l], ln2_b[l], seq)
        h = _ffn(alpha, l, x1.reshape(1, r, d), w_up, w_down, ln2_g[l], ln2_b[l], r)
```

```python
import functools
import math

import jax
import jax.numpy as jnp
from jax import lax
from jax.experimental import pallas as pl
from jax.experimental.pallas import tpu as pltpu

N_META = 16
N_GROUPS = 4
GROUP_DIM = 128
N_HEADS = 4
HEAD_DIM = 128
N_DIRS = 2
CONV_WIDTH = 5
CHUNK = 64
TAIL = 128
HALO = 16
LN_EPS = 1e-5
RMS_EPS = 1e-6
L2_EPS = 1e-6
FW = N_GROUPS * GROUP_DIM
DW = N_HEADS * HEAD_DIM
NGATE = N_DIRS * N_HEADS
VMEM_LIMIT = 56 * 1024 * 1024
DELTA_BATCHES_PER_STEP = 4
ROW_TILE = 1024
SUB_ROWS = 256

F32 = jnp.float32
BF16 = jnp.bfloat16


def _cparams(*sem):
    return pltpu.CompilerParams(dimension_semantics=sem, vmem_limit_bytes=VMEM_LIMIT)


def _dot(a, b):
    return jnp.dot(a, b, preferred_element_type=F32)


def _dot_nt(a, b):
    return lax.dot_general(a, b, (((1,), (1,)), ((), ())), preferred_element_type=F32)


def _dot_tn(a, b):
    return lax.dot_general(a, b, (((0,), (0,)), ((), ())), preferred_element_type=F32)


def _layer_norm(x, g, b):
    mu = jnp.mean(x, -1, keepdims=True)
    xc = x - mu
    var = jnp.mean(xc * xc, -1, keepdims=True)
    return xc * lax.rsqrt(var + LN_EPS) * g + b


def _sigmoid(x):
    return 1.0 / (1.0 + jnp.exp(-x))


def _softplus(x):
    return jnp.maximum(x, 0.0) + jnp.log1p(jnp.exp(-jnp.abs(x)))


def _pick_tile(n, target, mult):
    best = None
    for t in range(mult, min(n, target) + 1, mult):
        if n % t == 0:
            best = t
    assert best is not None, (n, target, mult)
    return best


def _resident(shape):
    nd = len(shape)
    return pl.BlockSpec(shape, lambda *_: (0,) * nd, pipeline_mode=pl.Buffered(1))


def _resident_layer(stack, layer):
    return pl.BlockSpec((None,) + stack.shape[1:], lambda *_: (layer, 0, 0), pipeline_mode=pl.Buffered(1))


def _ln0_kernel(x_ref, meta_ref, g_ref, b_ref, o_ref):
    j = pl.program_id(1)
    t0, d = x_ref.shape[1], x_ref.shape[2]
    seq = o_ref.shape[1] - TAIL
    o_ref[0, pl.ds(pl.multiple_of(j * t0, t0), t0), :] = _layer_norm(x_ref[0], g_ref[...], b_ref[...])

    @pl.when(j == 0)
    def _():
        o_ref[0, seq:seq + TAIL - N_META, :] = jnp.zeros((TAIL - N_META, d), F32)
        o_ref[0, seq + TAIL - N_META:, :] = _layer_norm(meta_ref[...], g_ref[...], b_ref[...])


def _input_norm(x, meta, g, b):
    bn, seq, d = x.shape
    lp = seq + TAIL
    t0 = _pick_tile(seq, 1024, 128)
    return pl.pallas_call(
        _ln0_kernel,
        grid=(bn, seq // t0),
        in_specs=[pl.BlockSpec((1, t0, d), lambda i, j: (i, j, 0)),
                  pl.BlockSpec((N_META, d), lambda i, j: (0, 0)),
                  pl.BlockSpec((1, d), lambda i, j: (0, 0)),
                  pl.BlockSpec((1, d), lambda i, j: (0, 0))],
        out_specs=pl.BlockSpec((1, lp, d), lambda i, j: (i, 0, 0)),
        out_shape=jax.ShapeDtypeStruct((bn, lp, d), F32),
        compiler_params=_cparams("parallel", "arbitrary"),
        name="ln0",
    )(x, meta, g.reshape(1, d), b.reshape(1, d))


def _inproj_kernel(h_ref, wu_ref, wqkv_ref, wgc_ref, wgr_ref, wz_ref, wg_ref, cg_ref, sg_ref,
                   uc_ref, us_ref, qkv_ref, gcol_ref, grow_ref, z_ref, gate_ref):
    x = h_ref[...].astype(BF16)
    u = _dot(x, wu_ref[...]).astype(BF16)
    half = 2 * GROUP_DIM
    for i in range(FW // half):
        ui = u[:, i * half:(i + 1) * half]
        uc_ref[:, i * half:(i + 1) * half] = _dot(ui, cg_ref[...]).astype(BF16)
        us_ref[:, i * half:(i + 1) * half] = _dot(ui, sg_ref[...]).astype(BF16)
    qkv_ref[...] = _dot(x, wqkv_ref[...]).astype(BF16)
    gcol_ref[...] = _dot(x, wgc_ref[...])
    gr = _dot_nt(wgr_ref[...], x)
    for c in range(grow_ref.shape[0]):
        grow_ref[c] = gr[:, c * CHUNK:(c + 1) * CHUNK]
    z_ref[...] = _dot(x, wz_ref[...]).astype(BF16)
    gate_ref[...] = _sigmoid(_dot(x, wg_ref[...])).astype(BF16)


def _in_projection(h, wb, cg2, sg2):
    r, d = h.shape
    tm = _pick_tile(r, ROW_TILE, 128)
    qw = 3 * DW
    o0, o1, o2, o3 = FW, FW + qw, FW + qw + 2 * NGATE, FW + qw + 2 * NGATE + DW
    wu, wqkv, wgc, wz, wg = wb[:, :o0], wb[:, o0:o1], wb[:, o1:o2], wb[:, o2:o3], wb[:, o3:]
    wgr = wgc.T
    gw = wg.shape[1]
    row = lambda n: pl.BlockSpec((tm, n), lambda i: (i, 0))
    outs = pl.pallas_call(
        _inproj_kernel,
        grid=(r // tm,),
        in_specs=[row(d), _resident(wu.shape), _resident(wqkv.shape), _resident(wgc.shape),
                  _resident(wgr.shape), _resident(wz.shape), _resident(wg.shape),
                  _resident(cg2.shape), _resident(sg2.shape)],
        out_specs=[row(FW), row(FW), row(qw), row(2 * NGATE),
                   pl.BlockSpec((tm // CHUNK, 2 * NGATE, CHUNK), lambda i: (i, 0, 0)),
                   row(DW), row(gw)],
        out_shape=[jax.ShapeDtypeStruct((r, FW), BF16), jax.ShapeDtypeStruct((r, FW), BF16),
                   jax.ShapeDtypeStruct((r, qw), BF16), jax.ShapeDtypeStruct((r, 2 * NGATE), F32),
                   jax.ShapeDtypeStruct((r // CHUNK, 2 * NGATE, CHUNK), F32),
                   jax.ShapeDtypeStruct((r, DW), BF16), jax.ShapeDtypeStruct((r, gw), BF16)],
        compiler_params=_cparams("parallel"),
        name="in_projection",
    )(h, wu, wqkv, wgc, wgr, wz, wg, cg2, sg2)
    return outs


def _half_rows(seq):
    return seq // 2 + 256


def _dft_tables(seq):
    n = seq + N_META
    hl = _half_rows(seq)
    nblk = hl // 128
    nxb = seq // 256
    m = jnp.arange(hl, dtype=jnp.int32)
    self_row = seq // 2 - N_META // 2
    pos = jnp.where(m < seq // 2, m + N_META, m - seq // 2)
    valid = (m <= self_row) | ((m >= seq // 2) & (m < seq // 2 + N_META))
    weight = jnp.where(valid, jnp.where(m == self_row, 0.5, 1.0), 0.0).astype(F32)
    pk = jnp.where(valid, pos, 0)
    blk = jnp.arange(nblk, dtype=jnp.int32)
    base = jnp.where(blk < nxb, N_META + 128 * blk, 0)
    w = 2.0 * math.pi / n
    pa = ((pk[:, None] * base[None, :]) % n).astype(F32) * w
    pb = ((pk[:, None] * jnp.arange(128, dtype=jnp.int32)[None, :]) % n).astype(F32) * w
    ca, sa, cb, sb = jnp.cos(pa), jnp.sin(pa), jnp.cos(pb), jnp.sin(pb)
    scale = (weight[:, None] * weight[None, :]) * n ** -0.5
    c = (ca[:, :, None] * cb[:, None, :] - sa[:, :, None] * sb[:, None, :]).reshape(hl, hl) * scale
    s = (sa[:, :, None] * cb[:, None, :] + ca[:, :, None] * sb[:, None, :]).reshape(hl, hl) * scale
    return c.astype(BF16), s.astype(BF16)


def _group_dft_mats():
    i = jnp.arange(GROUP_DIM, dtype=jnp.int32)
    ang = ((i[:, None] * i[None, :]) % GROUP_DIM).astype(F32) * (2.0 * math.pi / GROUP_DIM)
    eye2 = jnp.eye(2, dtype=F32)
    cg = jnp.kron(eye2, jnp.cos(ang)) * GROUP_DIM ** -0.5
    sg = jnp.kron(eye2, jnp.sin(ang)) * GROUP_DIM ** -0.5
    return cg.astype(BF16), sg.astype(BF16)


def _mirror_select():
    r = lax.broadcasted_iota(jnp.int32, (128, 256), 0)
    c = lax.broadcasted_iota(jnp.int32, (128, 256), 1)
    rm = lax.broadcasted_iota(jnp.int32, (N_META, 128), 0)
    cm = lax.broadcasted_iota(jnp.int32, (N_META, 128), 1)
    return (c == 240 - r).astype(BF16), ((cm == 128 - rm) & (rm > 0)).astype(BF16)


def _fold_kernel(a_ref, b_ref, e_ref, o_ref):
    lp = a_ref.shape[1]
    seq = lp - TAIL
    sel, sel_meta = _mirror_select()
    for src, dst, sign in ((a_ref, e_ref, 1.0), (b_ref, o_ref, -1.0)):
        for i in range(seq // 256):
            win = seq - 256 - 128 * i
            rev = _dot(sel, src[0, win:win + 256, :])
            dst[0, 128 * i:128 * (i + 1), :] = (src[0, 128 * i:128 * (i + 1), :].astype(F32) + sign * rev).astype(BF16)
        rev = _dot(sel_meta, src[0, seq - 128:seq, :])
        xh = seq // 2
        dst[0, xh:xh + N_META, :] = (src[0, lp - N_META:, :].astype(F32) + sign * rev).astype(BF16)
        dst[0, xh + N_META:, :] = jnp.zeros((dst.shape[1] - xh - N_META, dst.shape[2]), BF16)


def _unfold_kernel(d_ref, s_ref, y_ref):
    lp = y_ref.shape[1]
    seq = lp - TAIL
    fw = y_ref.shape[2]
    nxb = seq // 256
    xh = seq // 2
    sel, sel_meta = _mirror_select()
    for j in range(seq // 128):
        y = d_ref[0, 128 * j:128 * (j + 1), :].astype(F32) if j < nxb else None
        i1 = seq // 128 - 2 - j
        ok1, ok2 = 0 <= i1 < nxb, 0 <= i1 + 1 < nxb
        if ok1 and ok2:
            hi = _dot(sel, s_ref[0, 128 * i1:128 * i1 + 256, :])
        elif ok1:
            hi = _dot(sel[:, :128], s_ref[0, 128 * i1:128 * i1 + 128, :])
        elif ok2:
            hi = _dot(sel[:, 128:], s_ref[0, 128 * (i1 + 1):128 * (i1 + 2), :])
        else:
            hi = None
        if j == seq // 128 - 1:
            hi = hi + _dot_tn(sel_meta, s_ref[0, xh:xh + N_META, :])
        y = hi if y is None else (y if hi is None else y + hi)
        y_ref[0, 128 * j:128 * (j + 1), :] = y.astype(BF16)
    y_ref[0, seq:lp - N_META, :] = jnp.zeros((TAIL - N_META, fw), BF16)
    y_ref[0, lp - N_META:, :] = d_ref[0, xh:xh + N_META, :]


def _dft_kernel(c_ref, s_ref, e_ref, o_ref, d_ref, m_ref):
    c = c_ref[...]
    s = s_ref[...]
    for i in range(e_ref.shape[0]):
        p = _dot(c, e_ref[i])
        q = _dot(s, o_ref[i])
        d_ref[i] = (p - q).astype(BF16)
        m_ref[i] = (p + q).astype(BF16)


def _fourier_positions(ctab, stab, uc, us):
    bn, lp, fw = uc.shape
    hl = ctab.shape[0]
    whole = lambda n: pl.BlockSpec((1, n, fw), lambda i: (i, 0, 0))
    half = jax.ShapeDtypeStruct((bn, hl, fw), BF16)
    e, o = pl.pallas_call(
        _fold_kernel, grid=(bn,), in_specs=[whole(lp), whole(lp)], out_specs=[whole(hl), whole(hl)],
        out_shape=[half, half], compiler_params=_cparams("parallel"), name="fourier_fold",
    )(uc, us)
    nb = 2 if bn % 2 == 0 else 1
    tm = _pick_tile(hl, 768, 128)
    d, m = pl.pallas_call(
        _dft_kernel,
        grid=(bn // nb, hl // tm),
        in_specs=[pl.BlockSpec((tm, hl), lambda g, i: (i, 0)),
                  pl.BlockSpec((tm, hl), lambda g, i: (i, 0)),
                  pl.BlockSpec((nb, hl, fw), lambda g, i: (g, 0, 0)),
                  pl.BlockSpec((nb, hl, fw), lambda g, i: (g, 0, 0))],
        out_specs=[pl.BlockSpec((nb, tm, fw), lambda g, i: (g, i, 0))] * 2,
        out_shape=[half, half],
        compiler_params=_cparams("parallel", "parallel"),
        name="fourier_positions",
    )(ctab, stab, e, o)
    return pl.pallas_call(
        _unfold_kernel, grid=(bn,), in_specs=[whole(hl), whole(hl)], out_specs=whole(lp),
        out_shape=jax.ShapeDtypeStruct((bn, lp, fw), BF16), compiler_params=_cparams("parallel"),
        name="fourier_unfold",
    )(d, m)


def _conv_kernel(seq, qm_ref, qp_ref, qn_ref, gcol_ref, grow_ref, shift_ref, w_ref, ac_ref, dc_ref, ar_ref, dr_ref,
                 q_ref, k_ref, v_ref, gco_ref, gro_ref, xx_ref):
    tm = qm_ref.shape[1]
    j = pl.program_id(1)
    lp = seq + TAIL

    def pad_mask(start, n):
        p = start + lax.broadcasted_iota(jnp.int32, (n, 1), 0)
        p = jnp.where(p < 0, p + lp, jnp.where(p >= lp, p - lp, p))
        return (p < seq) | (p >= lp - N_META)

    live = pad_mask(j * tm, tm)
    zero = jnp.zeros((), BF16)
    xx_ref[:HALO, :] = jnp.where(pad_mask(j * tm - HALO, HALO), qp_ref[0], zero)
    xx_ref[HALO:HALO + tm, :] = jnp.where(live, qm_ref[0], zero)
    xx_ref[HALO + tm:2 * HALO + tm, :] = jnp.where(pad_mask(j * tm + tm, HALO), qn_ref[0], zero)
    xx_ref[2 * HALO + tm:, :] = jnp.zeros((xx_ref.shape[0] - 2 * HALO - tm, xx_ref.shape[1]), BF16)

    half = CONV_WIDTH // 2
    taps = [t for t in range(CONV_WIDTH) if t != half]
    wb = w_ref[...].astype(BF16)
    for part, dst in enumerate((q_ref, k_ref, v_ref)):
        cols = slice(part * DW, (part + 1) * DW)
        for i in range(tm // 128):
            shifted = _dot(shift_ref[...], xx_ref[128 * i:128 * i + 256, cols]).astype(BF16)
            y = xx_ref[HALO + 128 * i:HALO + 128 * (i + 1), cols] * wb[half:half + 1, cols]
            for n, t in enumerate(taps):
                y = y + shifted[128 * n:128 * (n + 1)] * wb[t:t + 1, cols]
            y = jnp.where(pad_mask(j * tm + 128 * i, 128), y * _sigmoid(y), zero)
            rows = slice(128 * i, 128 * (i + 1))
            if part == 2:
                dst[0, rows, :] = y
                continue
            scale = HEAD_DIM ** -0.5 if part == 0 else 1.0
            for h in range(N_HEADS):
                lo = h * HEAD_DIM
                yh = y[:, lo:lo + HEAD_DIM].astype(F32)
                yh = yh * (lax.rsqrt(jnp.sum(yh * yh, -1, keepdims=True) + L2_EPS) * scale)
                dst[0, rows, lo:lo + HEAD_DIM] = yh.astype(BF16)

    gl = gcol_ref[0]
    beta = jnp.where(live, _sigmoid(gl[:, :NGATE]), 0.0)
    g = jnp.where(live, -jnp.exp(ac_ref[...]) * _softplus(gl[:, NGATE:] + dc_ref[...]), 0.0)
    i0 = lax.broadcasted_iota(jnp.int32, (CHUNK, CHUNK), 0)
    i1 = lax.broadcasted_iota(jnp.int32, (CHUNK, CHUNK), 1)
    tri_lo = (i0 >= i1).astype(BF16)
    tri_up = (i0 <= i1).astype(BF16)
    is_fwd_col = lax.broadcasted_iota(jnp.int32, (1, NGATE), 1) < N_HEADS
    g1 = g.astype(BF16)
    g2 = (g - g1.astype(F32)).astype(BF16)
    g3 = (g - g1.astype(F32) - g2.astype(F32)).astype(BF16)
    for c in range(tm // CHUNK):
        sl = slice(c * CHUNK, (c + 1) * CHUNK)
        pre = _dot(tri_lo, g1[sl]) + _dot(tri_lo, g2[sl]) + _dot(tri_lo, g3[sl])
        suf = _dot(tri_up, g1[sl]) + _dot(tri_up, g2[sl]) + _dot(tri_up, g3[sl])
        gco_ref[0, sl, :NGATE] = beta[sl]
        gco_ref[0, sl, NGATE:] = jnp.where(is_fwd_col, pre, suf)

    nc = grow_ref.shape[0]
    glr = grow_ref[:, NGATE:, :]
    shp = (nc, NGATE, CHUNK)
    pr = j * tm + lax.broadcasted_iota(jnp.int32, shp, 0) * CHUNK + lax.broadcasted_iota(jnp.int32, shp, 2)
    live_r = (pr < seq) | (pr >= lp - N_META)
    gr = jnp.where(live_r, -jnp.exp(ar_ref[...])[None] * _softplus(glr + dr_ref[...][None]), 0.0)
    gr = gr.reshape(nc * NGATE, CHUNK)
    r1 = gr.astype(BF16)
    r2 = (gr - r1.astype(F32)).astype(BF16)
    r3 = (gr - r1.astype(F32) - r2.astype(F32)).astype(BF16)
    pre_r = _dot(r1, tri_up) + _dot(r2, tri_up) + _dot(r3, tri_up)
    suf_r = _dot(r1, tri_lo) + _dot(r2, tri_lo) + _dot(r3, tri_lo)
    is_fwd_row = lax.broadcasted_iota(jnp.int32, shp, 1) < N_HEADS
    gro_ref[...] = jnp.where(is_fwd_row, pre_r.reshape(shp), suf_r.reshape(shp))


def _conv_and_gates(qkv, gcol, grow, conv_w, a_log, dt_bias, seq):
    bn, lp, qw = qkv.shape
    tm = _pick_tile(lp, 1408, 128)
    nt = lp // tm
    nc = tm // CHUNK
    hb = tm // HALO
    nhb = lp // HALO
    a_flat = a_log.reshape(1, NGATE).astype(F32)
    d_flat = dt_bias.reshape(1, NGATE).astype(F32)
    small = lambda shape: pl.BlockSpec(shape, lambda i, j: (0,) * len(shape))
    tok = lambda n: pl.BlockSpec((1, tm, n), lambda i, j: (i, j, 0))
    half = CONV_WIDTH // 2
    rr = jnp.arange(128, dtype=jnp.int32)[:, None]
    cc = jnp.arange(256, dtype=jnp.int32)[None, :]
    shift = jnp.concatenate([(cc == rr + HALO + t - half) for t in range(CONV_WIDTH) if t != half], 0).astype(BF16)
    return pl.pallas_call(
        functools.partial(_conv_kernel, seq),
        grid=(bn, nt),
        in_specs=[tok(qw),
                  pl.BlockSpec((1, HALO, qw), lambda i, j: (i, (j * hb + nhb - 1) % nhb, 0)),
                  pl.BlockSpec((1, HALO, qw), lambda i, j: (i, ((j + 1) * hb) % nhb, 0)),
                  tok(2 * NGATE),
                  pl.BlockSpec((nc, 2 * NGATE, CHUNK), lambda i, j: (i * nt + j, 0, 0)),
                  small(shift.shape),
                  small((CONV_WIDTH, qw)), small((1, NGATE)), small((1, NGATE)),
                  small((NGATE, 1)), small((NGATE, 1))],
        out_specs=[tok(DW), tok(DW), tok(DW), tok(2 * NGATE),
                   pl.BlockSpec((nc, NGATE, CHUNK), lambda i, j: (i * nt + j, 0, 0))],
        out_shape=[jax.ShapeDtypeStruct((bn, lp, DW), BF16)] * 3
        + [jax.ShapeDtypeStruct((bn, lp, 2 * NGATE), F32),
           jax.ShapeDtypeStruct((bn * lp // CHUNK, NGATE, CHUNK), F32)],
        scratch_shapes=[pltpu.VMEM((tm + 128, qw), BF16)],
        compiler_params=_cparams("parallel", "parallel"),
        name="conv_gates",
    )(qkv, qkv, qkv, gcol, grow, shift, conv_w.astype(F32), a_flat, d_flat, a_flat.T, d_flat.T)


class _Chain:
    pass


def _delta_kernel(qf_ref, kf_ref, vf_ref, cf_ref, rf_ref, qb_ref, kb_ref, vb_ref, cb_ref, rb_ref,
                  of_ref, ob_ref, s_ref):
    step = pl.program_id(1)
    nb = qf_ref.shape[0]

    @pl.when(step == 0)
    def _():
        s_ref[...] = jnp.zeros_like(s_ref)

    i0 = lax.broadcasted_iota(jnp.int32, (CHUNK, CHUNK), 0)
    i1 = lax.broadcasted_iota(jnp.int32, (CHUNK, CHUNK), 1)
    eye = (i0 == i1).astype(F32)
    level_masks = []
    b = 1
    while b < CHUNK:
        level_masks.append((((i0 // (2 * b)) == (i1 // (2 * b))) & ((i0 // b) != (i1 // b))).astype(F32))
        b *= 2

    dirs = ((qf_ref, kf_ref, vf_ref, cf_ref, rf_ref, of_ref), (qb_ref, kb_ref, vb_ref, cb_ref, rb_ref, ob_ref))
    chains = []
    for bi in range(nb):
        for r, (q_ref, k_ref, v_ref, c_ref, r_ref, o_ref) in enumerate(dirs):
            incl = (i0 >= i1) if r == 0 else (i0 <= i1)
            strict = (i0 > i1) if r == 0 else (i0 < i1)
            last = CHUNK - 1 if r == 0 else 0
            gcol = c_ref[bi]
            grow = r_ref[bi, 0]
            for h in range(N_HEADS):
                ch = _Chain()
                c = r * N_HEADS + h
                lo = h * HEAD_DIM
                ch.o_ref, ch.bi, ch.lo, ch.slot = o_ref, bi, lo, bi * NGATE + c
                ch.q = q_ref[bi, :, lo:lo + HEAD_DIM]
                ch.k = k_ref[bi, :, lo:lo + HEAD_DIM]
                v = v_ref[bi, :, lo:lo + HEAD_DIM].astype(F32)
                kf32 = ch.k.astype(F32)
                beta = gcol[:, c:c + 1]
                gc = gcol[:, NGATE + c:NGATE + c + 1]
                gr = grow[c:c + 1, :]
                g_last = gcol[last:last + 1, NGATE + c:NGATE + c + 1]
                e_gc = jnp.exp(gc)
                ch.decay = jnp.exp(jnp.where(incl, gc - gr, -1e30))
                ch.decay_strict = jnp.where(strict, ch.decay, 0.0)
                kbeta = kf32 * beta
                ch.kbeta = kbeta.astype(BF16)
                ch.rhs = jnp.concatenate([v * beta, kbeta * e_gc], axis=-1).astype(BF16)
                ch.q_decay = (ch.q.astype(F32) * e_gc).astype(BF16)
                ch.k_tail = (kf32 * jnp.exp(g_last - gc)).astype(BF16)
                ch.chunk_decay = jnp.exp(g_last)
                chains.append(ch)

    for ch in chains:
        both = _dot_nt(jnp.concatenate([ch.kbeta, ch.q], axis=0), ch.k)
        ch.lmat = both[:CHUNK] * ch.decay_strict
        ch.attn = (both[CHUNK:] * ch.decay).astype(BF16)
        ch.x = eye - ch.lmat * level_masks[0]
    for m in level_masks[1:]:
        for ch in chains:
            ch.xb = ch.x.astype(BF16)
            ch.t = _dot(ch.xb, (ch.lmat * m).astype(BF16)).astype(BF16)
        for ch in chains:
            ch.x = ch.x - _dot(ch.t, ch.xb)
    for ch in chains:
        ch.sol = _dot(ch.x.astype(BF16), ch.rhs)
    for ch in chains:
        ch.s_old = s_ref[ch.slot]
        ch.sb = ch.s_old.astype(BF16)
        lhs = jnp.concatenate([ch.sol[:, HEAD_DIM:].astype(BF16), ch.q_decay], axis=0)
        ch.ks = _dot(lhs, ch.sb)
    for ch in chains:
        v_new = (ch.sol[:, :HEAD_DIM] - ch.ks[:CHUNK]).astype(BF16)
        o = ch.ks[CHUNK:] + _dot(ch.attn, v_new)
        s_ref[ch.slot] = ch.s_old * ch.chunk_decay + _dot_tn(ch.k_tail, v_new)
        ch.o_ref[ch.bi, :, ch.lo:ch.lo + HEAD_DIM] = o.astype(BF16)


def _delta_rule(q, k, v, gco, gro, seq):
    bn, lp, dw = q.shape
    nch = lp // CHUNK
    nx = seq // CHUNK
    ntail = nch - nx

    def fwd(s):
        return jnp.where(s < ntail, nch - 1 - s, s - ntail)

    def bwd(s):
        return jnp.where(s < nx, nx - 1 - s, s)

    nb = DELTA_BATCHES_PER_STEP if bn % DELTA_BATCHES_PER_STEP == 0 else 1
    gro = gro.reshape(bn, nch, NGATE, CHUNK)

    def tok(order):
        return pl.BlockSpec((nb, CHUNK, dw), lambda i, s: (i, order(s), 0))

    def col(order):
        return pl.BlockSpec((nb, CHUNK, 2 * NGATE), lambda i, s: (i, order(s), 0))

    def row(order):
        return pl.BlockSpec((nb, 1, NGATE, CHUNK), lambda i, s: (i, order(s), 0, 0))

    return pl.pallas_call(
        _delta_kernel,
        grid=(bn // nb, nch),
        in_specs=[tok(fwd), tok(fwd), tok(fwd), col(fwd), row(fwd),
                  tok(bwd), tok(bwd), tok(bwd), col(bwd), row(bwd)],
        out_specs=[tok(fwd), tok(bwd)],
        out_shape=[jax.ShapeDtypeStruct((bn, lp, dw), BF16)] * 2,
        scratch_shapes=[pltpu.VMEM((nb * NGATE, HEAD_DIM, HEAD_DIM), F32)],
        compiler_params=_cparams("parallel", "arbitrary"),
        name="delta_rule",
    )(q, k, v, gco, gro, q, k, v, gco, gro)


def _post_kernel(alpha, y_ref, of_ref, ob_ref, z_ref, gate_ref, h_ref, wf_ref, wd_ref, wo_ref,
                 ng_ref, g_ref, b_ref, o_ref):
    d = h_ref.shape[-1]
    tm = h_ref.shape[0]
    sub = _pick_tile(tm, SUB_ROWS, 128)
    ng = ng_ref[...]
    for s in range(tm // sub):
        rows = slice(s * sub, (s + 1) * sub)
        heads = []
        for hh in range(N_HEADS):
            sl = slice(hh * HEAD_DIM, (hh + 1) * HEAD_DIM)
            o = of_ref[rows, sl].astype(F32) + ob_ref[rows, sl].astype(F32)
            z = z_ref[rows, sl].astype(F32)
            o = o * lax.rsqrt(jnp.mean(o * o, -1, keepdims=True) + RMS_EPS) * ng * (z * _sigmoid(z))
            heads.append(o.astype(BF16))
        y_a = _dot(y_ref[rows, :], wf_ref[...])
        y_b = _dot(jnp.concatenate(heads, axis=-1), wd_ref[...])
        m = gate_ref[rows, :d].astype(F32) * y_a + gate_ref[rows, d:].astype(F32) * y_b
        mix = _dot(m.astype(BF16), wo_ref[...])
        o_ref[rows, :] = _layer_norm(alpha * h_ref[rows, :] + mix, g_ref[...], b_ref[...])


def _merge_and_project(alpha, layer, y, o_f, o_b, z, gates, h, wf, wd, wo, norm_g, ln_g, ln_b):
    r, d = h.shape
    tm = _pick_tile(r, ROW_TILE, 128)
    row = lambda n: pl.BlockSpec((tm, n), lambda i: (i, 0))
    return pl.pallas_call(
        functools.partial(_post_kernel, alpha),
        grid=(r // tm,),
        in_specs=[row(FW), row(DW), row(DW), row(DW), row(2 * d), row(d),
                  _resident_layer(wf, layer), _resident_layer(wd, layer), _resident_layer(wo, layer),
                  _resident((1, HEAD_DIM)), _resident((1, d)), _resident((1, d))],
        out_specs=row(d),
        out_shape=jax.ShapeDtypeStruct((r, d), F32),
        compiler_params=_cparams("parallel"),
        name="merge_project",
    )(y, o_f, o_b, z, gates, h, wf, wd, wo, norm_g.reshape(1, HEAD_DIM).astype(F32),
      ln_g.reshape(1, d).astype(F32), ln_b.reshape(1, d).astype(F32))


def _ffn_kernel(alpha, fchunk, x_ref, wu_ref, wd_ref, g_ref, b_ref, o_ref, acc_ref):
    x = x_ref[0]
    xb = x.astype(BF16)
    nf = wu_ref.shape[1] // fchunk
    for f in range(nf):
        sl = slice(f * fchunk, (f + 1) * fchunk)
        a = jnp.maximum(_dot(xb, wu_ref[:, sl]), 0.0)
        part = _dot((a * a).astype(BF16), wd_ref[sl, :])
        if f == 0:
            acc_ref[...] = part
        else:
            acc_ref[...] += part
    o_ref[0] = _layer_norm(alpha * x + acc_ref[...], g_ref[...], b_ref[...])


def _ffn(alpha, layer, x, wu, wd, ln_g, ln_b, rows_out):
    bn, lp, d = x.shape
    dff = wu.shape[2]
    tm = _pick_tile(rows_out, ROW_TILE, 128)
    tok = pl.BlockSpec((1, tm, d), lambda i, j: (i, j, 0))
    return pl.pallas_call(
        functools.partial(_ffn_kernel, alpha, min(dff, 512)),
        grid=(bn, rows_out // tm),
        in_specs=[tok, _resident_layer(wu, layer), _resident_layer(wd, layer), _resident((1, d)), _resident((1, d))],
        out_specs=tok,
        out_shape=jax.ShapeDtypeStruct((bn, rows_out, d), F32),
        scratch_shapes=[pltpu.VMEM((tm, d), F32)],
        compiler_params=_cparams("parallel", "parallel"),
        name="ffn",
    )(x, wu, wd, ln_g.reshape(1, d).astype(F32), ln_b.reshape(1, d).astype(F32))


def kernel(x, meta_tokens, ln0_g, ln0_b, w_in, conv_w, a_log, dt_bias, delta_norm_g, w_fourier_proj,
           w_delta_proj, w_out, ln1_g, ln1_b, w_up, w_down, ln2_g, ln2_b):
    bn, seq, d = x.shape
    depth = w_in.shape[0]
    assert seq % 256 == 0 and d == 8 * GROUP_DIM
    lp = seq + TAIL
    r = bn * lp
    alpha = (2 * depth) ** 0.25

    h = _input_norm(x, meta_tokens.astype(x.dtype), ln0_g, ln0_b)
    ctab, stab = _dft_tables(seq)
    cg2, sg2 = _group_dft_mats()
    w_in, w_fourier_proj, w_delta_proj, w_out, w_up, w_down = [
        w.astype(BF16) for w in (w_in, w_fourier_proj, w_delta_proj, w_out, w_up, w_down)]
    for l in range(depth):
        uc, us, qkv, gcol, grow, z, gates = _in_projection(h.reshape(r, d), w_in[l], cg2, sg2)
        y = _fourier_positions(ctab, stab, uc.reshape(bn, lp, FW), us.reshape(bn, lp, FW))
        q, k, v, gco, gro = _conv_and_gates(qkv.reshape(bn, lp, 3 * DW), gcol.reshape(bn, lp, 2 * NGATE),
                                            grow, conv_w[l], a_log[l], dt_bias[l], seq)
        o_f, o_b = _delta_rule(q, k, v, gco, gro, seq)
        x1 = _merge_and_project(alpha, l, y.reshape(r, FW), o_f.reshape(r, DW), o_b.reshape(r, DW), z, gates,
                                h.reshape(r, d), w_fourier_proj, w_delta_proj, w_out,
                                delta_norm_g[l], ln1_g[l], ln1_b[l])
        last = l == depth - 1
        if last:
            return _ffn(alpha, l, x1.reshape(bn, lp, d), w_up, w_down, ln2_g[l], ln2_b[l], seq)
        h = _ffn(alpha, l, x1.reshape(1, r, d), w_up, w_down, ln2_g[l], ln2_b[l], r)
```

```python
import functools
import math

import jax
import jax.numpy as jnp
from jax import lax
from jax.experimental import pallas as pl
from jax.experimental.pallas import tpu as pltpu

N_META = 16
N_GROUPS = 4
GROUP_DIM = 128
N_HEADS = 4
HEAD_DIM = 128
N_DIRS = 2
CONV_WIDTH = 5
CHUNK = 64
TAIL = 128
HALO = 16
LN_EPS = 1e-5
RMS_EPS = 1e-6
L2_EPS = 1e-6
FW = N_GROUPS * GROUP_DIM
DW = N_HEADS * HEAD_DIM
NGATE = N_DIRS * N_HEADS
VMEM_LIMIT = 56 * 1024 * 1024
DELTA_BATCHES_PER_STEP = 8
ROW_TILE = 1024
SUB_ROWS = 256

F32 = jnp.float32
BF16 = jnp.bfloat16


def _cparams(*sem):
    return pltpu.CompilerParams(dimension_semantics=sem, vmem_limit_bytes=VMEM_LIMIT)


def _dot(a, b):
    return jnp.dot(a, b, preferred_element_type=F32)


def _dot_nt(a, b):
    return lax.dot_general(a, b, (((1,), (1,)), ((), ())), preferred_element_type=F32)


def _dot_tn(a, b):
    return lax.dot_general(a, b, (((0,), (0,)), ((), ())), preferred_element_type=F32)


def _layer_norm(x, g, b):
    mu = jnp.mean(x, -1, keepdims=True)
    xc = x - mu
    var = jnp.mean(xc * xc, -1, keepdims=True)
    return xc * lax.rsqrt(var + LN_EPS) * g + b


def _sigmoid(x):
    return 1.0 / (1.0 + jnp.exp(-x))


def _softplus(x):
    return jnp.maximum(x, 0.0) + jnp.log1p(jnp.exp(-jnp.abs(x)))


def _pick_tile(n, target, mult):
    best = None
    for t in range(mult, min(n, target) + 1, mult):
        if n % t == 0:
            best = t
    assert best is not None, (n, target, mult)
    return best


def _resident(shape):
    nd = len(shape)
    return pl.BlockSpec(shape, lambda *_: (0,) * nd, pipeline_mode=pl.Buffered(1))


def _resident_layer(stack, layer):
    return pl.BlockSpec((None,) + stack.shape[1:], lambda *_: (layer, 0, 0), pipeline_mode=pl.Buffered(1))


def _ln0_kernel(x_ref, meta_ref, g_ref, b_ref, o_ref):
    j = pl.program_id(1)
    t0, d = x_ref.shape[1], x_ref.shape[2]
    seq = o_ref.shape[1] - TAIL
    o_ref[0, pl.ds(pl.multiple_of(j * t0, t0), t0), :] = _layer_norm(x_ref[0], g_ref[...], b_ref[...])

    @pl.when(j == 0)
    def _():
        o_ref[0, seq:seq + TAIL - N_META, :] = jnp.zeros((TAIL - N_META, d), F32)
        o_ref[0, seq + TAIL - N_META:, :] = _layer_norm(meta_ref[...], g_ref[...], b_ref[...])


def _input_norm(x, meta, g, b):
    bn, seq, d = x.shape
    lp = seq + TAIL
    t0 = _pick_tile(seq, 1024, 128)
    return pl.pallas_call(
        _ln0_kernel,
        grid=(bn, seq // t0),
        in_specs=[pl.BlockSpec((1, t0, d), lambda i, j: (i, j, 0)),
                  pl.BlockSpec((N_META, d), lambda i, j: (0, 0)),
                  pl.BlockSpec((1, d), lambda i, j: (0, 0)),
                  pl.BlockSpec((1, d), lambda i, j: (0, 0))],
        out_specs=pl.BlockSpec((1, lp, d), lambda i, j: (i, 0, 0)),
        out_shape=jax.ShapeDtypeStruct((bn, lp, d), F32),
        compiler_params=_cparams("parallel", "arbitrary"),
        name="ln0",
    )(x, meta, g.reshape(1, d), b.reshape(1, d))


def _inproj_kernel(h_ref, wu_ref, wqkv_ref, wgc_ref, wgr_ref, wz_ref, wg_ref, cg_ref, sg_ref,
                   uc_ref, us_ref, qkv_ref, gcol_ref, grow_ref, z_ref, gate_ref):
    x = h_ref[...].astype(BF16)
    u = _dot(x, wu_ref[...]).astype(BF16)
    half = 2 * GROUP_DIM
    for i in range(FW // half):
        ui = u[:, i * half:(i + 1) * half]
        uc_ref[:, i * half:(i + 1) * half] = _dot(ui, cg_ref[...]).astype(BF16)
        us_ref[:, i * half:(i + 1) * half] = _dot(ui, sg_ref[...]).astype(BF16)
    qkv_ref[...] = _dot(x, wqkv_ref[...]).astype(BF16)
    gcol_ref[...] = _dot(x, wgc_ref[...])
    gr = _dot_nt(wgr_ref[...], x)
    for c in range(grow_ref.shape[0]):
        grow_ref[c] = gr[:, c * CHUNK:(c + 1) * CHUNK]
    z_ref[...] = _dot(x, wz_ref[...]).astype(BF16)
    gate_ref[...] = _sigmoid(_dot(x, wg_ref[...])).astype(BF16)


def _in_projection(h, wb, cg2, sg2):
    r, d = h.shape
    tm = _pick_tile(r, ROW_TILE, 128)
    qw = 3 * DW
    o0, o1, o2, o3 = FW, FW + qw, FW + qw + 2 * NGATE, FW + qw + 2 * NGATE + DW
    wu, wqkv, wgc, wz, wg = wb[:, :o0], wb[:, o0:o1], wb[:, o1:o2], wb[:, o2:o3], wb[:, o3:]
    wgr = wgc.T
    gw = wg.shape[1]
    row = lambda n: pl.BlockSpec((tm, n), lambda i: (i, 0))
    outs = pl.pallas_call(
        _inproj_kernel,
        grid=(r // tm,),
        in_specs=[row(d), _resident(wu.shape), _resident(wqkv.shape), _resident(wgc.shape),
                  _resident(wgr.shape), _resident(wz.shape), _resident(wg.shape),
                  _resident(cg2.shape), _resident(sg2.shape)],
        out_specs=[row(FW), row(FW), row(qw), row(2 * NGATE),
                   pl.BlockSpec((tm // CHUNK, 2 * NGATE, CHUNK), lambda i: (i, 0, 0)),
                   row(DW), row(gw)],
        out_shape=[jax.ShapeDtypeStruct((r, FW), BF16), jax.ShapeDtypeStruct((r, FW), BF16),
                   jax.ShapeDtypeStruct((r, qw), BF16), jax.ShapeDtypeStruct((r, 2 * NGATE), F32),
                   jax.ShapeDtypeStruct((r // CHUNK, 2 * NGATE, CHUNK), F32),
                   jax.ShapeDtypeStruct((r, DW), BF16), jax.ShapeDtypeStruct((r, gw), BF16)],
        compiler_params=_cparams("parallel"),
        name="in_projection",
    )(h, wu, wqkv, wgc, wgr, wz, wg, cg2, sg2)
    return outs


def _half_rows(seq):
    return seq // 2 + 256


def _dft_tables(seq):
    n = seq + N_META
    hl = _half_rows(seq)
    nblk = hl // 128
    nxb = seq // 256
    m = jnp.arange(hl, dtype=jnp.int32)
    self_row = seq // 2 - N_META // 2
    pos = jnp.where(m < seq // 2, m + N_META, m - seq // 2)
    valid = (m <= self_row) | ((m >= seq // 2) & (m < seq // 2 + N_META))
    weight = jnp.where(valid, jnp.where(m == self_row, 0.5, 1.0), 0.0).astype(F32)
    pk = jnp.where(valid, pos, 0)
    blk = jnp.arange(nblk, dtype=jnp.int32)
    base = jnp.where(blk < nxb, N_META + 128 * blk, 0)
    w = 2.0 * math.pi / n
    pa = ((pk[:, None] * base[None, :]) % n).astype(F32) * w
    pb = ((pk[:, None] * jnp.arange(128, dtype=jnp.int32)[None, :]) % n).astype(F32) * w
    ca, sa, cb, sb = jnp.cos(pa), jnp.sin(pa), jnp.cos(pb), jnp.sin(pb)
    scale = (weight[:, None] * weight[None, :]) * n ** -0.5
    c = (ca[:, :, None] * cb[:, None, :] - sa[:, :, None] * sb[:, None, :]).reshape(hl, hl) * scale
    s = (sa[:, :, None] * cb[:, None, :] + ca[:, :, None] * sb[:, None, :]).reshape(hl, hl) * scale
    return c.astype(BF16), s.astype(BF16)


def _group_dft_mats():
    i = jnp.arange(GROUP_DIM, dtype=jnp.int32)
    ang = ((i[:, None] * i[None, :]) % GROUP_DIM).astype(F32) * (2.0 * math.pi / GROUP_DIM)
    eye2 = jnp.eye(2, dtype=F32)
    cg = jnp.kron(eye2, jnp.cos(ang)) * GROUP_DIM ** -0.5
    sg = jnp.kron(eye2, jnp.sin(ang)) * GROUP_DIM ** -0.5
    return cg.astype(BF16), sg.astype(BF16)


def _mirror_select():
    r = lax.broadcasted_iota(jnp.int32, (128, 256), 0)
    c = lax.broadcasted_iota(jnp.int32, (128, 256), 1)
    rm = lax.broadcasted_iota(jnp.int32, (N_META, 128), 0)
    cm = lax.broadcasted_iota(jnp.int32, (N_META, 128), 1)
    return (c == 240 - r).astype(BF16), ((cm == 128 - rm) & (rm > 0)).astype(BF16)


def _fold_kernel(a_ref, b_ref, e_ref, o_ref):
    lp = a_ref.shape[1]
    seq = lp - TAIL
    sel, sel_meta = _mirror_select()
    for src, dst, sign in ((a_ref, e_ref, 1.0), (b_ref, o_ref, -1.0)):
        for i in range(seq // 256):
            win = seq - 256 - 128 * i
            rev = _dot(sel, src[0, win:win + 256, :])
            dst[0, 128 * i:128 * (i + 1), :] = (src[0, 128 * i:128 * (i + 1), :].astype(F32) + sign * rev).astype(BF16)
        rev = _dot(sel_meta, src[0, seq - 128:seq, :])
        xh = seq // 2
        dst[0, xh:xh + N_META, :] = (src[0, lp - N_META:, :].astype(F32) + sign * rev).astype(BF16)
        dst[0, xh + N_META:, :] = jnp.zeros((dst.shape[1] - xh - N_META, dst.shape[2]), BF16)


def _unfold_kernel(d_ref, s_ref, y_ref):
    lp = y_ref.shape[1]
    seq = lp - TAIL
    fw = y_ref.shape[2]
    nxb = seq // 256
    xh = seq // 2
    sel, sel_meta = _mirror_select()
    for j in range(seq // 128):
        y = d_ref[0, 128 * j:128 * (j + 1), :].astype(F32) if j < nxb else None
        i1 = seq // 128 - 2 - j
        ok1, ok2 = 0 <= i1 < nxb, 0 <= i1 + 1 < nxb
        if ok1 and ok2:
            hi = _dot(sel, s_ref[0, 128 * i1:128 * i1 + 256, :])
        elif ok1:
            hi = _dot(sel[:, :128], s_ref[0, 128 * i1:128 * i1 + 128, :])
        elif ok2:
            hi = _dot(sel[:, 128:], s_ref[0, 128 * (i1 + 1):128 * (i1 + 2), :])
        else:
            hi = None
        if j == seq // 128 - 1:
            hi = hi + _dot_tn(sel_meta, s_ref[0, xh:xh + N_META, :])
        y = hi if y is None else (y if hi is None else y + hi)
        y_ref[0, 128 * j:128 * (j + 1), :] = y.astype(BF16)
    y_ref[0, seq:lp - N_META, :] = jnp.zeros((TAIL - N_META, fw), BF16)
    y_ref[0, lp - N_META:, :] = d_ref[0, xh:xh + N_META, :]


def _dft_kernel(c_ref, s_ref, e_ref, o_ref, d_ref, m_ref):
    c = c_ref[...]
    s = s_ref[...]
    for i in range(e_ref.shape[0]):
        p = _dot(c, e_ref[i])
        q = _dot(s, o_ref[i])
        d_ref[i] = (p - q).astype(BF16)
        m_ref[i] = (p + q).astype(BF16)


def _fourier_positions(ctab, stab, uc, us):
    bn, lp, fw = uc.shape
    hl = ctab.shape[0]
    whole = lambda n: pl.BlockSpec((1, n, fw), lambda i: (i, 0, 0))
    half = jax.ShapeDtypeStruct((bn, hl, fw), BF16)
    e, o = pl.pallas_call(
        _fold_kernel, grid=(bn,), in_specs=[whole(lp), whole(lp)], out_specs=[whole(hl), whole(hl)],
        out_shape=[half, half], compiler_params=_cparams("parallel"), name="fourier_fold",
    )(uc, us)
    nb = 2 if bn % 2 == 0 else 1
    tm = _pick_tile(hl, 768, 128)
    d, m = pl.pallas_call(
        _dft_kernel,
        grid=(bn // nb, hl // tm),
        in_specs=[pl.BlockSpec((tm, hl), lambda g, i: (i, 0)),
                  pl.BlockSpec((tm, hl), lambda g, i: (i, 0)),
                  pl.BlockSpec((nb, hl, fw), lambda g, i: (g, 0, 0)),
                  pl.BlockSpec((nb, hl, fw), lambda g, i: (g, 0, 0))],
        out_specs=[pl.BlockSpec((nb, tm, fw), lambda g, i: (g, i, 0))] * 2,
        out_shape=[half, half],
        compiler_params=_cparams("parallel", "parallel"),
        name="fourier_positions",
    )(ctab, stab, e, o)
    return pl.pallas_call(
        _unfold_kernel, grid=(bn,), in_specs=[whole(hl), whole(hl)], out_specs=whole(lp),
        out_shape=jax.ShapeDtypeStruct((bn, lp, fw), BF16), compiler_params=_cparams("parallel"),
        name="fourier_unfold",
    )(d, m)


def _conv_kernel(seq, qm_ref, qp_ref, qn_ref, gcol_ref, grow_ref, shift_ref, w_ref, ac_ref, dc_ref, ar_ref, dr_ref,
                 q_ref, k_ref, v_ref, gco_ref, gro_ref, xx_ref):
    tm = qm_ref.shape[1]
    j = pl.program_id(1)
    lp = seq + TAIL

    def pad_mask(start, n):
        p = start + lax.broadcasted_iota(jnp.int32, (n, 1), 0)
        p = jnp.where(p < 0, p + lp, jnp.where(p >= lp, p - lp, p))
        return (p < seq) | (p >= lp - N_META)

    def may_hold_pad(offset, n):
        return any(seq <= (jj * tm + offset + r) % lp < lp - N_META for jj in range(lp // tm) for r in range(n))

    def masked(offset, n, value, fill):
        return jnp.where(pad_mask(j * tm + offset, n), value, fill) if may_hold_pad(offset, n) else value

    live = pad_mask(j * tm, tm)
    zero = jnp.zeros((), BF16)
    xx_ref[:HALO, :] = masked(-HALO, HALO, qp_ref[0], zero)
    for i in range(tm // 128):
        xx_ref[HALO + 128 * i:HALO + 128 * (i + 1), :] = masked(128 * i, 128, qm_ref[0, 128 * i:128 * (i + 1), :], zero)
    xx_ref[HALO + tm:2 * HALO + tm, :] = masked(tm, HALO, qn_ref[0], zero)
    xx_ref[2 * HALO + tm:, :] = jnp.zeros((xx_ref.shape[0] - 2 * HALO - tm, xx_ref.shape[1]), BF16)

    half = CONV_WIDTH // 2
    taps = [t for t in range(CONV_WIDTH) if t != half]
    wb = w_ref[...].astype(BF16)
    for part, dst in enumerate((q_ref, k_ref, v_ref)):
        cols = slice(part * DW, (part + 1) * DW)
        for i in range(tm // 128):
            shifted = _dot(shift_ref[...], xx_ref[128 * i:128 * i + 256, cols]).astype(BF16)
            y = xx_ref[HALO + 128 * i:HALO + 128 * (i + 1), cols] * wb[half:half + 1, cols]
            for n, t in enumerate(taps):
                y = y + shifted[128 * n:128 * (n + 1)] * wb[t:t + 1, cols]
            y = masked(128 * i, 128, y * _sigmoid(y), zero)
            rows = slice(128 * i, 128 * (i + 1))
            if part == 2:
                dst[0, rows, :] = y
                continue
            scale = HEAD_DIM ** -0.5 if part == 0 else 1.0
            for h in range(N_HEADS):
                lo = h * HEAD_DIM
                yh = y[:, lo:lo + HEAD_DIM].astype(F32)
                yh = yh * (lax.rsqrt(jnp.sum(yh * yh, -1, keepdims=True) + L2_EPS) * scale)
                dst[0, rows, lo:lo + HEAD_DIM] = yh.astype(BF16)

    gl = gcol_ref[0]
    beta = jnp.where(live, _sigmoid(gl[:, :NGATE]), 0.0)
    g = jnp.where(live, -jnp.exp(ac_ref[...]) * _softplus(gl[:, NGATE:] + dc_ref[...]), 0.0)
    i0 = lax.broadcasted_iota(jnp.int32, (CHUNK, CHUNK), 0)
    i1 = lax.broadcasted_iota(jnp.int32, (CHUNK, CHUNK), 1)
    tri_lo = (i0 >= i1).astype(BF16)
    tri_up = (i0 <= i1).astype(BF16)
    is_fwd_col = lax.broadcasted_iota(jnp.int32, (1, NGATE), 1) < N_HEADS
    g1 = g.astype(BF16)
    g2 = (g - g1.astype(F32)).astype(BF16)
    g3 = (g - g1.astype(F32) - g2.astype(F32)).astype(BF16)
    for c in range(tm // CHUNK):
        sl = slice(c * CHUNK, (c + 1) * CHUNK)
        pre = _dot(tri_lo, g1[sl]) + _dot(tri_lo, g2[sl]) + _dot(tri_lo, g3[sl])
        suf = _dot(tri_up, g1[sl]) + _dot(tri_up, g2[sl]) + _dot(tri_up, g3[sl])
        gco_ref[0, sl, :NGATE] = beta[sl]
        gco_ref[0, sl, NGATE:] = jnp.where(is_fwd_col, pre, suf)

    nc = grow_ref.shape[0]
    glr = grow_ref[:, NGATE:, :]
    shp = (nc, NGATE, CHUNK)
    pr = j * tm + lax.broadcasted_iota(jnp.int32, shp, 0) * CHUNK + lax.broadcasted_iota(jnp.int32, shp, 2)
    live_r = (pr < seq) | (pr >= lp - N_META)
    gr = jnp.where(live_r, -jnp.exp(ar_ref[...])[None] * _softplus(glr + dr_ref[...][None]), 0.0)
    gr = gr.reshape(nc * NGATE, CHUNK)
    r1 = gr.astype(BF16)
    r2 = (gr - r1.astype(F32)).astype(BF16)
    r3 = (gr - r1.astype(F32) - r2.astype(F32)).astype(BF16)
    pre_r = _dot(r1, tri_up) + _dot(r2, tri_up) + _dot(r3, tri_up)
    suf_r = _dot(r1, tri_lo) + _dot(r2, tri_lo) + _dot(r3, tri_lo)
    is_fwd_row = lax.broadcasted_iota(jnp.int32, shp, 1) < N_HEADS
    gro_ref[...] = jnp.where(is_fwd_row, pre_r.reshape(shp), suf_r.reshape(shp))


def _conv_and_gates(qkv, gcol, grow, conv_w, a_log, dt_bias, seq):
    bn, lp, qw = qkv.shape
    tm = _pick_tile(lp, 1408, 128)
    nt = lp // tm
    nc = tm // CHUNK
    hb = tm // HALO
    nhb = lp // HALO
    a_flat = a_log.reshape(1, NGATE).astype(F32)
    d_flat = dt_bias.reshape(1, NGATE).astype(F32)
    small = lambda shape: pl.BlockSpec(shape, lambda i, j: (0,) * len(shape))
    tok = lambda n: pl.BlockSpec((1, tm, n), lambda i, j: (i, j, 0))
    half = CONV_WIDTH // 2
    rr = jnp.arange(128, dtype=jnp.int32)[:, None]
    cc = jnp.arange(256, dtype=jnp.int32)[None, :]
    shift = jnp.concatenate([(cc == rr + HALO + t - half) for t in range(CONV_WIDTH) if t != half], 0).astype(BF16)
    return pl.pallas_call(
        functools.partial(_conv_kernel, seq),
        grid=(bn, nt),
        in_specs=[tok(qw),
                  pl.BlockSpec((1, HALO, qw), lambda i, j: (i, (j * hb + nhb - 1) % nhb, 0)),
                  pl.BlockSpec((1, HALO, qw), lambda i, j: (i, ((j + 1) * hb) % nhb, 0)),
                  tok(2 * NGATE),
                  pl.BlockSpec((nc, 2 * NGATE, CHUNK), lambda i, j: (i * nt + j, 0, 0)),
                  small(shift.shape),
                  small((CONV_WIDTH, qw)), small((1, NGATE)), small((1, NGATE)),
                  small((NGATE, 1)), small((NGATE, 1))],
        out_specs=[tok(DW), tok(DW), tok(DW), tok(2 * NGATE),
                   pl.BlockSpec((nc, NGATE, CHUNK), lambda i, j: (i * nt + j, 0, 0))],
        out_shape=[jax.ShapeDtypeStruct((bn, lp, DW), BF16)] * 3
        + [jax.ShapeDtypeStruct((bn, lp, 2 * NGATE), F32),
           jax.ShapeDtypeStruct((bn * lp // CHUNK, NGATE, CHUNK), F32)],
        scratch_shapes=[pltpu.VMEM((tm + 128, qw), BF16)],
        compiler_params=_cparams("parallel", "parallel"),
        name="conv_gates",
    )(qkv, qkv, qkv, gcol, grow, shift, conv_w.astype(F32), a_flat, d_flat, a_flat.T, d_flat.T)


class _Chain:
    pass


def _delta_kernel(qf_ref, kf_ref, vf_ref, cf_ref, rf_ref, qb_ref, kb_ref, vb_ref, cb_ref, rb_ref,
                  of_ref, ob_ref, s_ref):
    step = pl.program_id(1)
    nb = qf_ref.shape[0]

    @pl.when(step == 0)
    def _():
        s_ref[...] = jnp.zeros_like(s_ref)

    i0 = lax.broadcasted_iota(jnp.int32, (CHUNK, CHUNK), 0)
    i1 = lax.broadcasted_iota(jnp.int32, (CHUNK, CHUNK), 1)
    eye = (i0 == i1).astype(F32)
    level_masks = []
    b = 1
    while b < CHUNK:
        level_masks.append((((i0 // (2 * b)) == (i1 // (2 * b))) & ((i0 // b) != (i1 // b))).astype(F32))
        b *= 2

    dirs = ((qf_ref, kf_ref, vf_ref, cf_ref, rf_ref, of_ref), (qb_ref, kb_ref, vb_ref, cb_ref, rb_ref, ob_ref))
    chains = []
    for bi in range(nb):
        for r, (q_ref, k_ref, v_ref, c_ref, r_ref, o_ref) in enumerate(dirs):
            incl = (i0 >= i1) if r == 0 else (i0 <= i1)
            strict = (i0 > i1) if r == 0 else (i0 < i1)
            last = CHUNK - 1 if r == 0 else 0
            gcol = c_ref[bi]
            grow = r_ref[bi, 0]
            for h in range(N_HEADS):
                ch = _Chain()
                c = r * N_HEADS + h
                lo = h * HEAD_DIM
                ch.o_ref, ch.bi, ch.lo, ch.slot = o_ref, bi, lo, bi * NGATE + c
                ch.q = q_ref[bi, :, lo:lo + HEAD_DIM]
                ch.k = k_ref[bi, :, lo:lo + HEAD_DIM]
                v = v_ref[bi, :, lo:lo + HEAD_DIM].astype(F32)
                kf32 = ch.k.astype(F32)
                beta = gcol[:, c:c + 1]
                gc = gcol[:, NGATE + c:NGATE + c + 1]
                gr = grow[c:c + 1, :]
                g_last = gcol[last:last + 1, NGATE + c:NGATE + c + 1]
                e_gc = jnp.exp(gc)
                ch.decay = jnp.exp(jnp.where(incl, gc - gr, -1e30))
                ch.decay_strict = jnp.where(strict, ch.decay, 0.0)
                kbeta = kf32 * beta
                ch.kbeta = kbeta.astype(BF16)
                ch.rhs = jnp.concatenate([v * beta, kbeta * e_gc], axis=-1).astype(BF16)
                ch.q_decay = (ch.q.astype(F32) * e_gc).astype(BF16)
                ch.k_tail = (kf32 * jnp.exp(g_last - gc)).astype(BF16)
                ch.chunk_decay = jnp.exp(g_last)
                chains.append(ch)

    for ch in chains:
        both = _dot_nt(jnp.concatenate([ch.kbeta, ch.q], axis=0), ch.k)
        ch.lmat = both[:CHUNK] * ch.decay_strict
        ch.attn = (both[CHUNK:] * ch.decay).astype(BF16)
        ch.x = eye - ch.lmat * level_masks[0]
    for m in level_masks[1:]:
        for ch in chains:
            ch.xb = ch.x.astype(BF16)
            ch.t = _dot(ch.xb, (ch.lmat * m).astype(BF16)).astype(BF16)
        for ch in chains:
            ch.x = ch.x - _dot(ch.t, ch.xb)
    for ch in chains:
        ch.sol = _dot(ch.x.astype(BF16), ch.rhs)
    for ch in chains:
        ch.s_old = s_ref[ch.slot]
        ch.sb = ch.s_old.astype(BF16)
        lhs = jnp.concatenate([ch.sol[:, HEAD_DIM:].astype(BF16), ch.q_decay], axis=0)
        ch.ks = _dot(lhs, ch.sb)
    for ch in chains:
        v_new = (ch.sol[:, :HEAD_DIM] - ch.ks[:CHUNK]).astype(BF16)
        o = ch.ks[CHUNK:] + _dot(ch.attn, v_new)
        s_ref[ch.slot] = ch.s_old * ch.chunk_decay + _dot_tn(ch.k_tail, v_new)
        ch.o_ref[ch.bi, :, ch.lo:ch.lo + HEAD_DIM] = o.astype(BF16)


def _delta_rule(q, k, v, gco, gro, seq):
    bn, lp, dw = q.shape
    nch = lp // CHUNK
    nx = seq // CHUNK
    ntail = nch - nx

    def fwd(s):
        return jnp.where(s < ntail, nch - 1 - s, s - ntail)

    def bwd(s):
        return jnp.where(s < nx, nx - 1 - s, s)

    nb = DELTA_BATCHES_PER_STEP if bn % DELTA_BATCHES_PER_STEP == 0 else 1
    gro = gro.reshape(bn, nch, NGATE, CHUNK)

    def tok(order):
        return pl.BlockSpec((nb, CHUNK, dw), lambda i, s: (i, order(s), 0))

    def col(order):
        return pl.BlockSpec((nb, CHUNK, 2 * NGATE), lambda i, s: (i, order(s), 0))

    def row(order):
        return pl.BlockSpec((nb, 1, NGATE, CHUNK), lambda i, s: (i, order(s), 0, 0))

    return pl.pallas_call(
        _delta_kernel,
        grid=(bn // nb, nch),
        in_specs=[tok(fwd), tok(fwd), tok(fwd), col(fwd), row(fwd),
                  tok(bwd), tok(bwd), tok(bwd), col(bwd), row(bwd)],
        out_specs=[tok(fwd), tok(bwd)],
        out_shape=[jax.ShapeDtypeStruct((bn, lp, dw), BF16)] * 2,
        scratch_shapes=[pltpu.VMEM((nb * NGATE, HEAD_DIM, HEAD_DIM), F32)],
        compiler_params=_cparams("parallel", "arbitrary"),
        name="delta_rule",
    )(q, k, v, gco, gro, q, k, v, gco, gro)


def _post_kernel(alpha, y_ref, of_ref, ob_ref, z_ref, gate_ref, h_ref, wf_ref, wd_ref, wo_ref,
                 ng_ref, g_ref, b_ref, o_ref):
    d = h_ref.shape[-1]
    tm = h_ref.shape[0]
    sub = _pick_tile(tm, SUB_ROWS, 128)
    ng = ng_ref[...]
    for s in range(tm // sub):
        rows = slice(s * sub, (s + 1) * sub)
        heads = []
        for hh in range(N_HEADS):
            sl = slice(hh * HEAD_DIM, (hh + 1) * HEAD_DIM)
            o = of_ref[rows, sl].astype(F32) + ob_ref[rows, sl].astype(F32)
            z = z_ref[rows, sl].astype(F32)
            o = o * lax.rsqrt(jnp.mean(o * o, -1, keepdims=True) + RMS_EPS) * ng * (z * _sigmoid(z))
            heads.append(o.astype(BF16))
        y_a = _dot(y_ref[rows, :], wf_ref[...])
        y_b = _dot(jnp.concatenate(heads, axis=-1), wd_ref[...])
        m = gate_ref[rows, :d].astype(F32) * y_a + gate_ref[rows, d:].astype(F32) * y_b
        mix = _dot(m.astype(BF16), wo_ref[...])
        o_ref[rows, :] = _layer_norm(alpha * h_ref[rows, :] + mix, g_ref[...], b_ref[...])


def _merge_and_project(alpha, layer, y, o_f, o_b, z, gates, h, wf, wd, wo, norm_g, ln_g, ln_b):
    r, d = h.shape
    tm = _pick_tile(r, ROW_TILE, 128)
    row = lambda n: pl.BlockSpec((tm, n), lambda i: (i, 0))
    return pl.pallas_call(
        functools.partial(_post_kernel, alpha),
        grid=(r // tm,),
        in_specs=[row(FW), row(DW), row(DW), row(DW), row(2 * d), row(d),
                  _resident_layer(wf, layer), _resident_layer(wd, layer), _resident_layer(wo, layer),
                  _resident((1, HEAD_DIM)), _resident((1, d)), _resident((1, d))],
        out_specs=row(d),
        out_shape=jax.ShapeDtypeStruct((r, d), F32),
        compiler_params=_cparams("parallel"),
        name="merge_project",
    )(y, o_f, o_b, z, gates, h, wf, wd, wo, norm_g.reshape(1, HEAD_DIM).astype(F32),
      ln_g.reshape(1, d).astype(F32), ln_b.reshape(1, d).astype(F32))


def _ffn_kernel(alpha, fchunk, x_ref, wu_ref, wd_ref, g_ref, b_ref, o_ref, acc_ref):
    x = x_ref[0]
    xb = x.astype(BF16)
    nf = wu_ref.shape[1] // fchunk
    for f in range(nf):
        sl = slice(f * fchunk, (f + 1) * fchunk)
        a = jnp.maximum(_dot(xb, wu_ref[:, sl]), 0.0)
        part = _dot((a * a).astype(BF16), wd_ref[sl, :])
        if f == 0:
            acc_ref[...] = part
        else:
            acc_ref[...] += part
    o_ref[0] = _layer_norm(alpha * x + acc_ref[...], g_ref[...], b_ref[...])


def _ffn(alpha, layer, x, wu, wd, ln_g, ln_b, rows_out):
    bn, lp, d = x.shape
    dff = wu.shape[2]
    tm = _pick_tile(rows_out, ROW_TILE, 128)
    tok = pl.BlockSpec((1, tm, d), lambda i, j: (i, j, 0))
    return pl.pallas_call(
        functools.partial(_ffn_kernel, alpha, min(dff, 512)),
        grid=(bn, rows_out // tm),
        in_specs=[tok, _resident_layer(wu, layer), _resident_layer(wd, layer), _resident((1, d)), _resident((1, d))],
        out_specs=tok,
        out_shape=jax.ShapeDtypeStruct((bn, rows_out, d), F32),
        scratch_shapes=[pltpu.VMEM((tm, d), F32)],
        compiler_params=_cparams("parallel", "parallel"),
        name="ffn",
    )(x, wu, wd, ln_g.reshape(1, d).astype(F32), ln_b.reshape(1, d).astype(F32))


def kernel(x, meta_tokens, ln0_g, ln0_b, w_in, conv_w, a_log, dt_bias, delta_norm_g, w_fourier_proj,
           w_delta_proj, w_out, ln1_g, ln1_b, w_up, w_down, ln2_g, ln2_b):
    bn, seq, d = x.shape
    depth = w_in.shape[0]
    assert seq % 256 == 0 and d == 8 * GROUP_DIM
    lp = seq + TAIL
    r = bn * lp
    alpha = (2 * depth) ** 0.25

    h = _input_norm(x, meta_tokens.astype(x.dtype), ln0_g, ln0_b)
    ctab, stab = _dft_tables(seq)
    cg2, sg2 = _group_dft_mats()
    w_in, w_fourier_proj, w_delta_proj, w_out, w_up, w_down = [
        w.astype(BF16) for w in (w_in, w_fourier_proj, w_delta_proj, w_out, w_up, w_down)]
    for l in range(depth):
        uc, us, qkv, gcol, grow, z, gates = _in_projection(h.reshape(r, d), w_in[l], cg2, sg2)
        y = _fourier_positions(ctab, stab, uc.reshape(bn, lp, FW), us.reshape(bn, lp, FW))
        q, k, v, gco, gro = _conv_and_gates(qkv.reshape(bn, lp, 3 * DW), gcol.reshape(bn, lp, 2 * NGATE),
                                            grow, conv_w[l], a_log[l], dt_bias[l], seq)
        o_f, o_b = _delta_rule(q, k, v, gco, gro, seq)
        x1 = _merge_and_project(alpha, l, y.reshape(r, FW), o_f.reshape(r, DW), o_b.reshape(r, DW), z, gates,
                                h.reshape(r, d), w_fourier_proj, w_delta_proj, w_out,
                                delta_norm_g[l], ln1_g[l], ln1_b[l])
        last = l == depth - 1
        if last:
            return _ffn(alpha, l, x1.reshape(bn, lp, d), w_up, w_down, ln2_g[l], ln2_b[l], seq)
        h = _ffn(alpha, l, x1.reshape(1, r, d), w_up, w_down, ln2_g[l], ln2_b[l], r)
```
